```python
import math
import jax, jax.numpy as jnp
from jax import lax
import numpy as np

D_MODEL = 2048
BATCH = 2
SEQ = 4096
DEPTH = 4
DEC_BATCH = 8
DEC_SEQ = 1
PAST_LEN = 16384
PAGE_SIZE = 128

MIX_WIDTH = D_MODEL
D_ATTN = MIX_WIDTH // 2
N_HEADS = 8
DV = D_ATTN // N_HEADS
DQK = DV // 2
N_MAPS = 2 * N_HEADS
D_Q = N_MAPS * DQK
ROT_DIM = DQK // 4
ROPE_THETA = 500000.0
D_LRU = MIX_WIDTH - D_ATTN
N_LRU_BLOCKS = 8
LRU_BW = D_LRU // N_LRU_BLOCKS
CONV_W = 4
LRU_C = 8.0
D_FF = 4 * D_MODEL
D_IN = 2 * D_Q + D_ATTN + 2 * D_LRU
Q_BLOCK = 128
EPS = 1e-6
SUBLN_EPS = 1e-5
NEG_INF = -1e30

kernel_name = 'hymba_diffattn_rglru_decode_step'


def rmsnorm(x, g, eps=EPS):
    xf = x.astype(jnp.float32)
    y = xf * lax.rsqrt(jnp.mean(xf * xf, axis=-1, keepdims=True) + eps)
    return (y * g.astype(jnp.float32)).astype(x.dtype)


def rope_partial(x, positions):
    inv_freq = ROPE_THETA ** (-jnp.arange(0, ROT_DIM, 2, dtype=jnp.float32) / ROT_DIM)
    ang = positions.astype(jnp.float32)[:, None] * inv_freq[None, :]
    cos = jnp.cos(ang)[None, :, None, :]
    sin = jnp.sin(ang)[None, :, None, :]
    xf = x.astype(jnp.float32)
    x1 = xf[..., :ROT_DIM // 2]
    x2 = xf[..., ROT_DIM // 2:ROT_DIM]
    out = jnp.concatenate([x1 * cos - x2 * sin, x2 * cos + x1 * sin, xf[..., ROT_DIM:]], axis=-1)
    return out.astype(x.dtype)


def prompt_diff_attention(q, k, v):
    B, S = q.shape[0], q.shape[1]
    nb = S // Q_BLOCK
    scale = DQK ** -0.5
    qb = q.reshape(B, nb, Q_BLOCK, N_MAPS, DQK).swapaxes(0, 1)
    kpos = jnp.arange(S)

    def block(args):
        qi, bi = args
        s = jnp.einsum('bqhd,bkhd->bhqk', qi, k).astype(jnp.float32) * scale
        qpos = bi * Q_BLOCK + jnp.arange(Q_BLOCK)
        s = jnp.where(kpos[None, :] <= qpos[:, None], s, NEG_INF)
        p = jax.nn.softmax(s, axis=-1).astype(v.dtype)
        p = p.reshape(B, N_HEADS, 2, Q_BLOCK, S)
        return jnp.einsum('bhcqk,bkhd->bqhcd', p, v)

    o = lax.map(block, (qb, jnp.arange(nb)))
    return o.swapaxes(0, 1).reshape(B, S, N_HEADS, 2, DV)


def sample_diff_attention(q, k, v, k_past, v_past):
    T = q.shape[1]
    P = k_past.shape[1]
    B = q.shape[0]
    scale = DQK ** -0.5
    s_past = jnp.einsum('bqhd,bkhd->bhqk', q, k_past).astype(jnp.float32) * scale
    s_new = jnp.einsum('bqhd,bkhd->bhqk', q, k).astype(jnp.float32) * scale
    causal = jnp.arange(T)[None, :] <= jnp.arange(T)[:, None]
    s_new = jnp.where(causal, s_new, NEG_INF)
    p = jax.nn.softmax(jnp.concatenate([s_past, s_new], axis=-1), axis=-1).astype(v.dtype)
    p = p.reshape(B, N_HEADS, 2, T, P + T)
    return (jnp.einsum('bhcqk,bkhd->bqhcd', p[..., :P], v_past)
            + jnp.einsum('bhcqk,bkhd->bqhcd', p[..., P:], v))


def diff_combine(o, lam, g, lambda_init):
    B, S = o.shape[0], o.shape[1]
    of = o.astype(jnp.float32)
    d = of[..., 0, :] - lam * of[..., 1, :]
    d = d * lax.rsqrt(jnp.mean(d * d, axis=-1, keepdims=True) + SUBLN_EPS)
    d = d * g.astype(jnp.float32) * (1.0 - lambda_init)
    return d.reshape(B, S, D_ATTN).astype(o.dtype)


def rglru(xc, h0, w_a, b_a, w_x, b_x, lam):
    B, S = xc.shape[0], xc.shape[1]
    xf = xc.astype(jnp.float32)
    xb = xf.reshape(B, S, N_LRU_BLOCKS, LRU_BW)
    r = jax.nn.sigmoid(jnp.einsum('bshi,hij->bshj', xb, w_a.astype(jnp.float32)).reshape(B, S, D_LRU)
                       + b_a.astype(jnp.float32))
    i = jax.nn.sigmoid(jnp.einsum('bshi,hij->bshj', xb, w_x.astype(jnp.float32)).reshape(B, S, D_LRU)
                       + b_x.astype(jnp.float32))
    log_a = -LRU_C * r * jax.nn.softplus(-lam.astype(jnp.float32))
    a = jnp.exp(log_a)
    u = jnp.sqrt(-jnp.expm1(2.0 * log_a)) * (i * xf)

    def step(h, au):
        a_t, u_t = au
        h = a_t * h + u_t
        return h, h

    hT, hs = lax.scan(step, h0.astype(jnp.float32), (a.swapaxes(0, 1), u.swapaxes(0, 1)))
    return hs.swapaxes(0, 1), hT


def mixer(xn, l, positions, conv_buf, h0, attend, P):
    B, S = xn.shape[0], xn.shape[1]
    z = xn @ P['w_in'][l]
    q = z[..., :D_Q].reshape(B, S, N_MAPS, DQK)
    k = z[..., D_Q:2 * D_Q].reshape(B, S, N_MAPS, DQK)
    v = z[..., 2 * D_Q:2 * D_Q + D_ATTN].reshape(B, S, N_HEADS, DV)
    xr = z[..., 2 * D_Q + D_ATTN:2 * D_Q + D_ATTN + D_LRU]
    gr = z[..., 2 * D_Q + D_ATTN + D_LRU:]
    q = rope_partial(q, positions)
    k = rope_partial(k, positions)
    lambda_init = 0.8 - 0.6 * math.exp(-0.3 * l)
    lam = (jnp.exp(jnp.sum(P['lambda_q1'][l].astype(jnp.float32) * P['lambda_k1'][l].astype(jnp.float32)))
           - jnp.exp(jnp.sum(P['lambda_q2'][l].astype(jnp.float32) * P['lambda_k2'][l].astype(jnp.float32)))
           + lambda_init)
    o = attend(l, q, k, v)
    attn_out = diff_combine(o, lam, P['subln_g'][l], lambda_init)
    xp = jnp.concatenate([conv_buf.astype(xr.dtype), xr], axis=1)
    cw = P['conv_w'][l]
    xc = P['conv_b'][l] + sum(cw[j] * xp[:, j:j + S] for j in range(CONV_W))
    new_buf = xp[:, S:]
    hs, hT = rglru(xc, h0, P['w_gate_a'][l], P['b_gate_a'][l], P['w_gate_x'][l], P['b_gate_x'][l],
                   P['lru_lambda'][l])
    lru_out = (hs * jax.nn.gelu(gr.astype(jnp.float32))).astype(xn.dtype)
    mix = jnp.concatenate([attn_out, lru_out], axis=-1) @ P['w_out'][l]
    return mix, k, v, hT, new_buf


def trunk(x, positions, conv_bufs, h0s, attend, P):
    ks, vs, hts, bufs = [], [], [], []
    for l in range(DEPTH):
        mix, k, v, hT, buf = mixer(rmsnorm(x, P['norm_mix_g'][l]), l, positions, conv_bufs[l], h0s[l],
                                   attend, P)
        x = x + mix
        hm = rmsnorm(x, P['norm_mlp_g'][l])
        x = x + jnp.square(jax.nn.relu(hm @ P['w_up'][l])) @ P['w_down'][l]
        ks.append(k)
        vs.append(v)
        hts.append(hT.astype(x.dtype))
        bufs.append(buf)
    y = rmsnorm(x, P['final_norm_g'])
    return y, jnp.stack(ks), jnp.stack(vs), jnp.stack(hts), jnp.stack(bufs)


def setup_inputs(seed: int = 0) -> dict:
    key = jax.random.key(seed)
    ks = jax.random.split(key, 26)
    n_pages = PAST_LEN // PAGE_SIZE
    n_used = DEC_BATCH * n_pages
    n_pool = n_used + max(1, n_used // 4)
    f32 = jnp.float32
    nrm = lambda k, shape, s: jax.random.normal(k, shape, f32) * s
    page_table = jax.random.permutation(ks[0], n_pool)[:n_used].reshape(DEC_BATCH, n_pages).astype(jnp.int32)
    radius = jax.random.uniform(ks[1], (DEPTH, D_LRU), f32, 0.9, 0.999)
    lru_lambda = jnp.log(radius) - jnp.log1p(-radius)
    return {
        'x_prompt': nrm(ks[2], (BATCH, SEQ, D_MODEL), 1.0),
        'x_sample': nrm(ks[3], (DEC_BATCH, DEC_SEQ, D_MODEL), 1.0),
        'cache_k': nrm(ks[4], (DEPTH, n_pool, PAGE_SIZE, N_MAPS, DQK), 1.0),
        'cache_v': nrm(ks[5], (DEPTH, n_pool, PAGE_SIZE, N_HEADS, DV), 1.0),
        'state_h': nrm(ks[6], (DEPTH, DEC_BATCH, D_LRU), 0.5),
        'state_conv': nrm(ks[7], (DEPTH, DEC_BATCH, CONV_W - 1, D_LRU), 1.0),
        'page_table': page_table,
        'norm_mix_g': 1.0 + nrm(ks[8], (DEPTH, D_MODEL), 0.02),
        'w_in': nrm(ks[9], (DEPTH, D_MODEL, D_IN), D_MODEL ** -0.5),
        'lambda_q1': nrm(ks[10], (DEPTH, DQK), 0.1),
        'lambda_k1': nrm(ks[11], (DEPTH, DQK), 0.1),
        'lambda_q2': nrm(ks[12], (DEPTH, DQK), 0.1),
        'lambda_k2': nrm(ks[13], (DEPTH, DQK), 0.1),
        'subln_g': 1.0 + nrm(ks[14], (DEPTH, DV), 0.02),
        'conv_w': nrm(ks[15], (DEPTH, CONV_W, D_LRU), CONV_W ** -0.5),
        'conv_b': nrm(ks[16], (DEPTH, D_LRU), 0.01),
        'w_gate_a': nrm(ks[17], (DEPTH, N_LRU_BLOCKS, LRU_BW, LRU_BW), LRU_BW ** -0.5),
        'b_gate_a': nrm(ks[18], (DEPTH, D_LRU), 0.1),
        'w_gate_x': nrm(ks[19], (DEPTH, N_LRU_BLOCKS, LRU_BW, LRU_BW), LRU_BW ** -0.5),
        'b_gate_x': nrm(ks[20], (DEPTH, D_LRU), 0.1),
        'lru_lambda': lru_lambda,
        'w_out': nrm(ks[21], (DEPTH, MIX_WIDTH, D_MODEL), MIX_WIDTH ** -0.5),
        'norm_mlp_g': 1.0 + nrm(ks[22], (DEPTH, D_MODEL), 0.02),
        'w_up': nrm(ks[23], (DEPTH, D_MODEL, D_FF), D_MODEL ** -0.5),
        'w_down': nrm(ks[24], (DEPTH, D_FF, D_MODEL), D_FF ** -0.5),
        'final_norm_g': 1.0 + nrm(ks[25], (D_MODEL,), 0.02),
    }


def reference(x_prompt, x_sample, cache_k, cache_v, state_h, state_conv, page_table,
              norm_mix_g, w_in, lambda_q1, lambda_k1, lambda_q2, lambda_k2, subln_g,
              conv_w, conv_b, w_gate_a, b_gate_a, w_gate_x, b_gate_x, lru_lambda,
              w_out, norm_mlp_g, w_up, w_down, final_norm_g):
    P = {'norm_mix_g': norm_mix_g, 'w_in': w_in, 'lambda_q1': lambda_q1, 'lambda_k1': lambda_k1,
         'lambda_q2': lambda_q2, 'lambda_k2': lambda_k2, 'subln_g': subln_g, 'conv_w': conv_w,
         'conv_b': conv_b, 'w_gate_a': w_gate_a, 'b_gate_a': b_gate_a, 'w_gate_x': w_gate_x,
         'b_gate_x': b_gate_x, 'lru_lambda': lru_lambda, 'w_out': w_out, 'norm_mlp_g': norm_mlp_g,
         'w_up': w_up, 'w_down': w_down, 'final_norm_g': final_norm_g}

    B, S = x_prompt.shape[0], x_prompt.shape[1]
    pos_prompt = jnp.arange(S, dtype=jnp.int32)
    zero_bufs = jnp.zeros((DEPTH, B, CONV_W - 1, D_LRU), x_prompt.dtype)
    zero_h = jnp.zeros((DEPTH, B, D_LRU), jnp.float32)

    def attend_prompt(l, q, k, v):
        return prompt_diff_attention(q, k, v)

    y_prompt, k_prompt, v_prompt, h_prompt, conv_prompt = trunk(
        x_prompt, pos_prompt, zero_bufs, zero_h, attend_prompt, P)

    DB, T = x_sample.shape[0], x_sample.shape[1]
    n_pages = page_table.shape[1]
    past_len = n_pages * cache_k.shape[2]
    pos_sample = past_len + jnp.arange(T, dtype=jnp.int32)

    def attend_sample(l, q, k, v):
        k_past = cache_k[l, page_table].reshape(DB, past_len, N_MAPS, DQK)
        v_past = cache_v[l, page_table].reshape(DB, past_len, N_HEADS, DV)
        return sample_diff_attention(q, k, v, k_past.astype(q.dtype), v_past.astype(v.dtype))

    y_sample, k_sample, v_sample, h_sample, conv_sample = trunk(
        x_sample, pos_sample, state_conv, state_h, attend_sample, P)

    return (y_prompt, y_sample, k_prompt, v_prompt, h_prompt, conv_prompt,
            k_sample, v_sample, h_sample, conv_sample)
```

```python
import functools
import math

import jax
import jax.numpy as jnp
from jax import lax
from jax.experimental import pallas as pl
from jax.experimental.pallas import tpu as pltpu

F32 = jnp.float32
BF16 = jnp.bfloat16

LANES_V7X = 128
VMEM_LIMIT_BYTES_V7X = 56 * 1024 * 1024

N_HEADS = 8
ROPE_THETA = 500000.0
ROT_FRACTION = 4
LRU_C = 8.0
EPS = 1e-6
SUBLN_EPS = 1e-5
NEG_INF = -1e30
PAGES_PER_STEP = 8


def _params(*sem):
    return pltpu.CompilerParams(dimension_semantics=sem, vmem_limit_bytes=VMEM_LIMIT_BYTES_V7X)


def _rms(y, g, eps):
    return y * lax.rsqrt(jnp.mean(y * y, axis=-1, keepdims=True) + eps) * g


def _rmsnorm_kernel(x_ref, g_ref, o_ref):
    o_ref[...] = _rms(x_ref[...], g_ref[...], EPS).astype(o_ref.dtype)


def _rmsnorm(x, g_all, layer, bm):
    m, d = x.shape
    return pl.pallas_call(
        _rmsnorm_kernel,
        grid=(m // bm,),
        in_specs=[pl.BlockSpec((bm, d), lambda i: (i, 0)),
                  pl.BlockSpec((None, 1, d), lambda i: (layer, 0, 0))],
        out_specs=pl.BlockSpec((bm, d), lambda i: (i, 0)),
        out_shape=jax.ShapeDtypeStruct((m, d), BF16),
        compiler_params=_params("arbitrary"),
        name="rmsnorm",
    )(x, g_all)


def _in_proj_kernel(xn_ref, w_ref, c_ref, s1_ref, s2_ref,
                    q_ref, k_ref, v_ref, xr_ref, gr_ref, *, q_scale):
    j = pl.program_id(1)
    z = jnp.dot(xn_ref[...], w_ref[...], preferred_element_type=F32)

    def rope_store(dst_ref, mul):
        c, s1, s2 = c_ref[...], s1_ref[...], s2_ref[...]
        for t in range(dst_ref.shape[1] // LANES_V7X):
            sl = slice(t * LANES_V7X, (t + 1) * LANES_V7X)
            zt = z[:, sl]
            r = (zt * c + pltpu.roll(zt, LANES_V7X - 8, 1) * s1) + pltpu.roll(zt, 8, 1) * s2
            dst_ref[:, sl] = (r * mul).astype(dst_ref.dtype)

    @pl.when(j == 0)
    def _():
        rope_store(q_ref, q_scale)

    @pl.when(j == 1)
    def _():
        rope_store(k_ref, 1.0)

    @pl.when(j == 2)
    def _():
        v_ref[...] = z

    @pl.when(j == 3)
    def _():
        xr_ref[...] = z

    @pl.when(j == 4)
    def _():
        gr_ref[...] = z


def _in_proj(xn, w_all, layer, tables, bm, q_scale):
    m, d = xn.shape
    width = w_all.shape[2] // 5
    c, s1, s2 = tables
    n_tbl = c.shape[0] // bm
    tbl_spec = pl.BlockSpec((bm, LANES_V7X), lambda i, j: (i % n_tbl, 0))
    out_spec = pl.BlockSpec((bm, width), lambda i, j: (i, 0))
    f32_out = jax.ShapeDtypeStruct((m, width), F32)
    return pl.pallas_call(
        functools.partial(_in_proj_kernel, q_scale=q_scale),
        grid=(m // bm, 5),
        in_specs=[pl.BlockSpec((bm, d), lambda i, j: (i, 0)),
                  pl.BlockSpec((None, d, width), lambda i, j: (layer, 0, j)),
                  tbl_spec, tbl_spec, tbl_spec],
        out_specs=[out_spec] * 5,
        out_shape=[jax.ShapeDtypeStruct((m, width), BF16), f32_out, f32_out, f32_out, f32_out],
        compiler_params=_params("arbitrary", "arbitrary"),
        name="in_proj",
    )(xn, w_all, c, s1, s2)


def _diff_lambda(lam_ref, lambda_init):
    lv = lam_ref[...]
    l1 = jnp.sum(lv[0:1] * lv[1:2], axis=1, keepdims=True)
    l2 = jnp.sum(lv[2:3] * lv[3:4], axis=1, keepdims=True)
    return jnp.exp(l1) - jnp.exp(l2) + lambda_init


def _diff_combine(o0, o1, lam, g, lambda_init):
    d = o0 - lam * o1
    d = d * lax.rsqrt(jnp.mean(d * d, axis=-1, keepdims=True) + SUBLN_EPS)
    return d * g * (1.0 - lambda_init)


def _attn_kernel(q_ref, k_ref, v_ref, lam_ref, g_ref, o_ref,
                 kt_ref, vb_ref, m_ref, l_ref, acc_ref, *, bq, dqk, lambda_init):
    qi = pl.program_id(2)
    nk = kt_ref.shape[0]

    @pl.when(qi == 0)
    def _():
        for j in range(nk):
            kt_ref[j] = k_ref[j * bq:(j + 1) * bq, :].T.astype(BF16)
        vb_ref[...] = v_ref[...].astype(BF16)

    q = q_ref[...]
    lane = lax.broadcasted_iota(jnp.int32, q.shape, 1)
    zero = jnp.zeros_like(q)
    q2 = jnp.concatenate([jnp.where(lane < dqk, q, zero), jnp.where(lane >= dqk, q, zero)], axis=0)

    m_ref[...] = jnp.full(m_ref.shape, NEG_INF, F32)
    l_ref[...] = jnp.zeros(l_ref.shape, F32)
    acc_ref[...] = jnp.zeros(acc_ref.shape, F32)

    def block(kj, masked):
        s = jnp.dot(q2, kt_ref[kj], preferred_element_type=F32)
        if masked:
            row = lax.broadcasted_iota(jnp.int32, s.shape, 0) & (bq - 1)
            col = lax.broadcasted_iota(jnp.int32, s.shape, 1)
            s = jnp.where(col <= row, s, NEG_INF)
        m_prev = m_ref[...]
        m_new = jnp.maximum(m_prev, jnp.max(s, axis=1, keepdims=True))
        alpha = jnp.exp(m_prev - m_new)
        p = jnp.exp(s - m_new)
        l_ref[...] = alpha * l_ref[...] + jnp.sum(p, axis=1, keepdims=True)
        start = pl.multiple_of(kj * bq, bq)
        pv = jnp.dot(p.astype(BF16), vb_ref[pl.ds(start, bq), :], preferred_element_type=F32)
        acc_ref[...] = alpha * acc_ref[...] + pv
        m_ref[...] = m_new

    def body(kj, carry):
        block(kj, False)
        return carry

    lax.fori_loop(0, qi, body, 0)
    block(qi, True)

    o = acc_ref[...] / l_ref[...]
    lam = _diff_lambda(lam_ref, lambda_init)
    o_ref[...] = _diff_combine(o[:bq], o[bq:], lam, g_ref[...], lambda_init).astype(o_ref.dtype)


def _prompt_attention(q, k, v, lam_all, g_all, layer, batch, seq, bq, lambda_init):
    m, width = q.shape
    dv = width // N_HEADS
    dqk = dv // 2
    nq = seq // bq
    assert (bq & (bq - 1)) == 0 and seq % bq == 0
    return pl.pallas_call(
        functools.partial(_attn_kernel, bq=bq, dqk=dqk, lambda_init=lambda_init),
        grid=(batch, N_HEADS, nq),
        in_specs=[pl.BlockSpec((bq, dv), lambda b, h, i: (b * nq + i, h)),
                  pl.BlockSpec((seq, dv), lambda b, h, i: (b, h)),
                  pl.BlockSpec((seq, dv), lambda b, h, i: (b, h)),
                  pl.BlockSpec((None, 4, dqk), lambda b, h, i: (layer, 0, 0)),
                  pl.BlockSpec((None, 1, dv), lambda b, h, i: (layer, 0, 0))],
        out_specs=pl.BlockSpec((bq, dv), lambda b, h, i: (b * nq + i, h)),
        out_shape=jax.ShapeDtypeStruct((m, width), BF16),
        scratch_shapes=[pltpu.VMEM((nq, dv, bq), BF16),
                        pltpu.VMEM((seq, dv), BF16),
                        pltpu.VMEM((2 * bq, 1), F32),
                        pltpu.VMEM((2 * bq, 1), F32),
                        pltpu.VMEM((2 * bq, dv), F32)],
        compiler_params=_params("arbitrary", "arbitrary", "arbitrary"),
        name="prompt_attention",
    )(q, k, v, lam_all, g_all)


def _decode_attn_kernel(pt_ref, q_ref, kn_ref, vn_ref, lam_ref, g_ref, *rest,
                        n_maps, lambda_init):
    del pt_ref
    k_refs = rest[:PAGES_PER_STEP]
    v_refs = rest[PAGES_PER_STEP:2 * PAGES_PER_STEP]
    o_ref, qm_ref, m_ref, l_ref, acc_ref = rest[2 * PAGES_PER_STEP:]
    p_idx = pl.program_id(1)
    width = q_ref.shape[-1]
    dqk = width // n_maps
    dv = 2 * dqk

    row = lax.broadcasted_iota(jnp.int32, (n_maps, width), 0)
    lane = lax.broadcasted_iota(jnp.int32, (n_maps, width), 1)
    own_qk = (lane >= row * dqk) & (lane < (row + 1) * dqk)

    @pl.when(p_idx == 0)
    def _():
        qrow = jnp.broadcast_to(q_ref[0].astype(F32), (n_maps, width))
        qm = jnp.where(own_qk, qrow, 0.0)
        qm_ref[...] = qm.astype(BF16)
        kn = jnp.broadcast_to(kn_ref[0].astype(BF16).astype(F32), (n_maps, width))
        s_new = jnp.sum(qm * kn, axis=1, keepdims=True)
        m_ref[...] = s_new
        l_ref[...] = jnp.ones(l_ref.shape, F32)
        vn = vn_ref[0].astype(BF16).astype(F32)
        acc_ref[...] = jnp.broadcast_to(vn, (n_maps, width))

    qm = qm_ref[...]
    for kp_ref, vp_ref in zip(k_refs, v_refs):
        s = lax.dot_general(qm, kp_ref[...].astype(BF16), (((1,), (1,)), ((), ())),
                            preferred_element_type=F32)
        m_prev = m_ref[...]
        m_new = jnp.maximum(m_prev, jnp.max(s, axis=1, keepdims=True))
        alpha = jnp.exp(m_prev - m_new)
        p = jnp.exp(s - m_new)
        l_ref[...] = alpha * l_ref[...] + jnp.sum(p, axis=1, keepdims=True)
        pv = jnp.dot(p.astype(BF16), vp_ref[...].astype(BF16), preferred_element_type=F32)
        acc_ref[...] = alpha * acc_ref[...] + pv
        m_ref[...] = m_new

    @pl.when(p_idx == pl.num_programs(1) - 1)
    def _():
        o = acc_ref[...] / l_ref[...]
        head = lax.shift_right_logical(lane, dv.bit_length() - 1)
        o0 = jnp.sum(jnp.where(row == 2 * head, o, 0.0), axis=0, keepdims=True)
        o1 = jnp.sum(jnp.where(row == 2 * head + 1, o, 0.0), axis=0, keepdims=True)
        lam = _diff_lambda(lam_ref, lambda_init)
        g = g_ref[...]
        for h in range(width // dv):
            sl = slice(h * dv, (h + 1) * dv)
            o_ref[0, :, sl] = _diff_combine(o0[:, sl], o1[:, sl], lam, g, lambda_init).astype(o_ref.dtype)


def _decode_attention(q, k_new, v_new, cache_k, cache_v, page_table, lam_all, g_all, layer,
                      lambda_init):
    nb, width = q.shape
    depth, n_pool, page, n_maps, dqk = cache_k.shape
    dv = cache_v.shape[-1]
    n_pages = page_table.shape[1]
    assert n_pages % PAGES_PER_STEP == 0
    ck = cache_k.reshape(depth, n_pool, page, width)
    cv = cache_v.reshape(depth, n_pool, page, width)
    row3 = lambda a: a.reshape(nb, 1, width)

    def page_spec(i):
        return pl.BlockSpec((None, None, page, width),
                            lambda b, p, pt: (layer, pt[b, p * PAGES_PER_STEP + i], 0, 0))

    row_spec = pl.BlockSpec((1, 1, width), lambda b, p, pt: (b, 0, 0))
    grid_spec = pltpu.PrefetchScalarGridSpec(
        num_scalar_prefetch=1,
        grid=(nb, n_pages // PAGES_PER_STEP),
        in_specs=[row_spec, row_spec, row_spec,
                  pl.BlockSpec((None, 4, dqk), lambda b, p, pt: (layer, 0, 0)),
                  pl.BlockSpec((None, 1, dv), lambda b, p, pt: (layer, 0, 0))]
                 + [page_spec(i) for i in range(PAGES_PER_STEP)] * 2,
        out_specs=row_spec,
        scratch_shapes=[pltpu.VMEM((n_maps, width), BF16),
                        pltpu.VMEM((n_maps, 1), F32),
                        pltpu.VMEM((n_maps, 1), F32),
                        pltpu.VMEM((n_maps, width), F32)],
    )
    out = pl.pallas_call(
        functools.partial(_decode_attn_kernel, n_maps=n_maps, lambda_init=lambda_init),
        grid_spec=grid_spec,
        out_shape=jax.ShapeDtypeStruct((nb, 1, width), BF16),
        compiler_params=_params("arbitrary", "arbitrary"),
        name="decode_attention",
    )(page_table, row3(q), row3(k_new), row3(v_new), lam_all, g_all,
      *([ck] * PAGES_PER_STEP), *([cv] * PAGES_PER_STEP))
    return out.reshape(nb, width)


def _lru_gates(xc, wg_ref, ba_ref, bx_ref, sp, store):
    xcb = xc.astype(BF16)
    bw = wg_ref.shape[1]
    for hb in range(wg_ref.shape[0]):
        sl = slice(hb * bw, (hb + 1) * bw)
        y = jnp.dot(xcb[:, sl], wg_ref[hb], preferred_element_type=F32)
        r = jax.nn.sigmoid(y[:, :bw] + ba_ref[:, sl])
        i = jax.nn.sigmoid(y[:, bw:] + bx_ref[:, sl])
        log_a = -LRU_C * r * sp[:, sl]
        a = jnp.exp(log_a)
        u = jnp.sqrt(-jnp.tanh(log_a) * (a * a + 1.0)) * (i * xc[:, sl])
        store(sl, a, u)


def _lru_kernel(xr_ref, gr_ref, conv0_ref, h0_ref, cw_ref, cb_ref, wg_ref, ba_ref, bx_ref, lam_ref,
                out_ref, ht_ref, cv_ref, xp_ref, a_ref, u_ref, h_ref, *, ts):
    ti = pl.program_id(1)
    taps = cw_ref.shape[0]
    pad = 8
    hist = pad - (taps - 1)

    @pl.when(ti == 0)
    def _():
        xp_ref[hist:pad, :] = conv0_ref[0]
        h_ref[...] = h0_ref[0]

    xp_ref[pad:pad + ts, :] = xr_ref[...]
    cw = cw_ref[...]
    conv = cw[0:1] * xp_ref[hist:hist + ts, :]
    for j in range(1, taps):
        conv = conv + cw[j:j + 1] * xp_ref[hist + j:hist + j + ts, :]
    xc = cb_ref[...] + conv
    sp = jax.nn.softplus(-lam_ref[...])

    def store(sl, a, u):
        a_ref[:, sl] = a
        u_ref[:, sl] = u

    _lru_gates(xc, wg_ref, ba_ref, bx_ref, sp, store)

    def step(t, h):
        h = a_ref[pl.ds(t, 1), :] * h + u_ref[pl.ds(t, 1), :]
        u_ref[pl.ds(t, 1), :] = h
        return h

    h_last = lax.fori_loop(0, ts, step, h_ref[...], unroll=8)
    h_ref[...] = h_last
    ht_ref[0] = h_last
    tail = xp_ref[hist + ts:pad + ts, :]
    cv_ref[0] = tail
    xp_ref[hist:pad, :] = tail
    out_ref[...] = (u_ref[...] * jax.nn.gelu(gr_ref[...])).astype(out_ref.dtype)


def _lru_weight_specs(layer, taps, d, nblk, bw, nargs):
    idx = {1: lambda b: (layer, 0, 0), 2: lambda b, t: (layer, 0, 0)}[nargs]
    idx4 = {1: lambda b: (layer, 0, 0, 0), 2: lambda b, t: (layer, 0, 0, 0)}[nargs]
    vec = pl.BlockSpec((None, 1, d), idx)
    return [pl.BlockSpec((None, taps, d), idx), vec,
            pl.BlockSpec((None, nblk, bw, 2 * bw), idx4), vec, vec, vec]


def _prompt_lru(xr, gr, conv0, h0, lw, layer, batch, seq, ts):
    m, d = xr.shape
    nt = seq // ts
    taps = lw[0].shape[1]
    nblk, bw = lw[2].shape[1], lw[2].shape[2]
    tile = pl.BlockSpec((ts, d), lambda b, t: (b * nt + t, 0))
    return pl.pallas_call(
        functools.partial(_lru_kernel, ts=ts),
        grid=(batch, nt),
        in_specs=[tile, tile,
                  pl.BlockSpec((1, taps - 1, d), lambda b, t: (b, 0, 0)),
                  pl.BlockSpec((1, 1, d), lambda b, t: (b, 0, 0))]
                 + _lru_weight_specs(layer, taps, d, nblk, bw, 2),
        out_specs=[tile,
                   pl.BlockSpec((1, 1, d), lambda b, t: (b, 0, 0)),
                   pl.BlockSpec((1, taps - 1, d), lambda b, t: (b, 0, 0))],
        out_shape=[jax.ShapeDtypeStruct((m, d), BF16),
                   jax.ShapeDtypeStruct((batch, 1, d), F32),
                   jax.ShapeDtypeStruct((batch, taps - 1, d), F32)],
        scratch_shapes=[pltpu.VMEM((ts + 8, d), F32),
                        pltpu.VMEM((ts, d), F32),
                        pltpu.VMEM((ts, d), F32),
                        pltpu.VMEM((1, d), F32)],
        compiler_params=_params("arbitrary", "arbitrary"),
        name="prompt_lru",
    )(xr, gr, conv0, h0.reshape(batch, 1, d), *lw)


def _sample_lru_kernel(xr_ref, gr_ref, conv_ref, h0_ref, cw_ref, cb_ref, wg_ref, ba_ref, bx_ref,
                       lam_ref, out_ref, ht_ref, cv_ref):
    taps = cw_ref.shape[0]
    xr = xr_ref[...]
    cw = cw_ref[...]
    conv = cw[0:1] * conv_ref[0]
    for j in range(1, taps - 1):
        conv = conv + cw[j:j + 1] * conv_ref[j]
    conv = conv + cw[taps - 1:taps] * xr
    xc = cb_ref[...] + conv
    sp = jax.nn.softplus(-lam_ref[...])
    h0 = h0_ref[...]
    gate = jax.nn.gelu(gr_ref[...])

    def store(sl, a, u):
        h = a * h0[:, sl] + u
        ht_ref[:, sl] = h
        out_ref[:, sl] = (h * gate[:, sl]).astype(out_ref.dtype)

    _lru_gates(xc, wg_ref, ba_ref, bx_ref, sp, store)
    for j in range(taps - 2):
        cv_ref[j] = conv_ref[j + 1]
    cv_ref[taps - 2] = xr


def _sample_lru(xr, gr, conv_t, h0, lw, layer):
    nb, d = xr.shape
    taps = lw[0].shape[1]
    nblk, bw = lw[2].shape[1], lw[2].shape[2]
    row = pl.BlockSpec((nb, d), lambda b: (0, 0))
    hist = pl.BlockSpec((taps - 1, nb, d), lambda b: (0, 0, 0))
    return pl.pallas_call(
        _sample_lru_kernel,
        grid=(1,),
        in_specs=[row, row, hist, row] + _lru_weight_specs(layer, taps, d, nblk, bw, 1),
        out_specs=[row, row, hist],
        out_shape=[jax.ShapeDtypeStruct((nb, d), BF16),
                   jax.ShapeDtypeStruct((nb, d), F32),
                   jax.ShapeDtypeStruct((taps - 1, nb, d), F32)],
        compiler_params=_params("arbitrary"),
        name="sample_lru",
    )(xr, gr, conv_t, h0, *lw)


def _out_proj_kernel(a_ref, r_ref, w_ref, x_ref, g_ref, xo_ref, xn_ref):
    half = a_ref.shape[1]
    y = jnp.dot(a_ref[...], w_ref[:half, :], preferred_element_type=F32)
    y = y + jnp.dot(r_ref[...], w_ref[half:, :], preferred_element_type=F32)
    x = x_ref[...] + y
    xo_ref[...] = x
    xn_ref[...] = _rms(x, g_ref[...], EPS).astype(xn_ref.dtype)


def _out_proj(attn, lru, w_all, x, g_all, layer, bm):
    m, half = attn.shape
    d = x.shape[1]
    return pl.pallas_call(
        _out_proj_kernel,
        grid=(m // bm,),
        in_specs=[pl.BlockSpec((bm, half), lambda i: (i, 0)),
                  pl.BlockSpec((bm, half), lambda i: (i, 0)),
                  pl.BlockSpec((None, 2 * half, d), lambda i: (layer, 0, 0)),
                  pl.BlockSpec((bm, d), lambda i: (i, 0)),
                  pl.BlockSpec((None, 1, d), lambda i: (layer, 0, 0))],
        out_specs=[pl.BlockSpec((bm, d), lambda i: (i, 0)),
                   pl.BlockSpec((bm, d), lambda i: (i, 0))],
        out_shape=[jax.ShapeDtypeStruct((m, d), F32), jax.ShapeDtypeStruct((m, d), BF16)],
        compiler_params=_params("arbitrary"),
        name="out_proj",
    )(attn, lru, w_all, x, g_all)


def _mlp_up_kernel(xn_ref, w_ref, h_ref):
    y = jnp.dot(xn_ref[...], w_ref[...], preferred_element_type=F32)
    h_ref[...] = jnp.square(jnp.maximum(y, 0.0)).astype(h_ref.dtype)


def _mlp_up(xn, w_all, layer, bm, bn):
    m, d = xn.shape
    f = w_all.shape[2]
    return pl.pallas_call(
        _mlp_up_kernel,
        grid=(m // bm, f // bn),
        in_specs=[pl.BlockSpec((bm, d), lambda i, j: (i, 0)),
                  pl.BlockSpec((None, d, bn), lambda i, j: (layer, 0, j))],
        out_specs=pl.BlockSpec((bm, bn), lambda i, j: (i, j)),
        out_shape=jax.ShapeDtypeStruct((m, f), BF16),
        compiler_params=_params("arbitrary", "arbitrary"),
        name="mlp_up",
    )(xn, w_all)


def _mlp_down_kernel(h_ref, w_ref, x_ref, g_ref, xo_ref, xn_ref, acc_ref):
    kk = pl.program_id(1)
    y = jnp.dot(h_ref[...], w_ref[...], preferred_element_type=F32)

    @pl.when(kk == 0)
    def _():
        acc_ref[...] = y

    @pl.when(kk > 0)
    def _():
        acc_ref[...] += y

    @pl.when(kk == pl.num_programs(1) - 1)
    def _():
        x = x_ref[...] + acc_ref[...]
        xo_ref[...] = x
        xn_ref[...] = _rms(x, g_ref[...], EPS).astype(xn_ref.dtype)


def _mlp_down(h, w_all, layer, x, g, norm_dtype, bm, bk):
    m, f = h.shape
    d = x.shape[1]
    return pl.pallas_call(
        _mlp_down_kernel,
        grid=(m // bm, f // bk),
        in_specs=[pl.BlockSpec((bm, bk), lambda i, k: (i, k)),
                  pl.BlockSpec((None, bk, d), lambda i, k: (layer, k, 0)),
                  pl.BlockSpec((bm, d), lambda i, k: (i, 0)),
                  pl.BlockSpec((1, d), lambda i, k: (0, 0))],
        out_specs=[pl.BlockSpec((bm, d), lambda i, k: (i, 0)),
                   pl.BlockSpec((bm, d), lambda i, k: (i, 0))],
        out_shape=[jax.ShapeDtypeStruct((m, d), F32), jax.ShapeDtypeStruct((m, d), norm_dtype)],
        scratch_shapes=[pltpu.VMEM((bm, d), F32)],
        compiler_params=_params("arbitrary", "arbitrary"),
        name="mlp_down",
    )(h, w_all, x, g)


def _rope_tables(positions, dqk):
    rot = dqk // ROT_FRACTION
    inv_freq = ROPE_THETA ** (-jnp.arange(0, rot, 2, dtype=F32) / rot)
    ang = positions.astype(F32)[:, None] * inv_freq[None, :]
    cos, sin = jnp.cos(ang), jnp.sin(ang)
    n = positions.shape[0]
    zh = jnp.zeros((n, rot // 2), F32)
    rest0 = jnp.zeros((n, dqk - rot), F32)
    c = jnp.concatenate([cos, cos, jnp.ones((n, dqk - rot), F32)], axis=1)
    s1 = jnp.concatenate([-sin, zh, rest0], axis=1)
    s2 = jnp.concatenate([zh, sin, rest0], axis=1)
    reps = LANES_V7X // dqk
    return tuple(jnp.tile(t, (1, reps)) for t in (c, s1, s2))


def _trunk(x, tables, conv0, h0, P, attend, lru, bm, bm_down, bk_down):
    depth = P["w_in"].shape[0]
    dqk = P["lam"].shape[2]
    q_scale = dqk ** -0.5
    xn = _rmsnorm(x, P["norm_mix_g"], 0, bm)
    ks, vs, hts, bufs = [], [], [], []
    y = None
    for l in range(depth):
        lambda_init = 0.8 - 0.6 * math.exp(-0.3 * l)
        q, k, v, xr, gr = _in_proj(xn, P["w_in"], l, tables, bm_down, q_scale)
        attn = attend(l, q, k, v, lambda_init)
        lru_out, ht, buf = lru(l, xr, gr, conv0[l], h0[l])
        x, xn = _out_proj(attn, lru_out, P["w_out"], x, P["norm_mlp_g"], l, bm_down)
        hmid = _mlp_up(xn, P["w_up"], l, bm, 1024)
        last = l == depth - 1
        g_next = P["final_norm_g"] if last else P["norm_mix_g"][l + 1]
        x, xn = _mlp_down(hmid, P["w_down"], l, x, g_next, F32 if last else BF16, bm_down, bk_down)
        y = xn
        ks.append(k)
        vs.append(v)
        hts.append(ht)
        bufs.append(buf)
    return y, jnp.stack(ks), jnp.stack(vs), jnp.stack(hts), jnp.stack(bufs)


def kernel(x_prompt, x_sample, cache_k, cache_v, state_h, state_conv, page_table, norm_mix_g, w_in, lambda_q1, lambda_k1, lambda_q2, lambda_k2, subln_g, conv_w, conv_b, w_gate_a, b_gate_a, w_gate_x, b_gate_x, lru_lambda, w_out, norm_mlp_g, w_up, w_down, final_norm_g):
    batch, seq, d_model = x_prompt.shape
    nb, t_new, _ = x_sample.shape
    depth, _, page, n_maps, dqk = cache_k.shape
    dv = cache_v.shape[-1]
    d_lru = state_h.shape[-1]
    taps = conv_w.shape[1]
    assert t_new == 1, "the sample kernels handle one new token per sequence"
    past_len = page_table.shape[1] * page

    vec = lambda a: a.reshape(depth, 1, a.shape[-1])
    P = {
        "w_in": w_in.astype(BF16), "w_out": w_out.astype(BF16),
        "w_up": w_up.astype(BF16), "w_down": w_down.astype(BF16),
        "norm_mix_g": vec(norm_mix_g), "norm_mlp_g": vec(norm_mlp_g),
        "final_norm_g": final_norm_g.reshape(1, d_model),
        "lam": jnp.stack([lambda_q1, lambda_k1, lambda_q2, lambda_k2], axis=1),
        "subln_g": vec(subln_g),
    }
    lw = (conv_w, vec(conv_b), jnp.concatenate([w_gate_a, w_gate_x], axis=-1).astype(BF16),
          vec(b_gate_a), vec(b_gate_x), vec(lru_lambda))

    bm = 1024
    tables = _rope_tables(jnp.arange(seq, dtype=jnp.int32), dqk)

    def attend_prompt(l, q, k, v, lambda_init):
        return _prompt_attention(q, k, v, P["lam"], P["subln_g"], l, batch, seq, 512, lambda_init)

    def lru_prompt(l, xr, gr, conv0, h0):
        return _prompt_lru(xr, gr, conv0, h0, lw, l, batch, seq, 512)

    zero_conv = jnp.zeros((depth, batch, taps - 1, d_lru), F32)
    zero_h = jnp.zeros((depth, batch, d_lru), F32)
    y_p, k_p, v_p, h_p, c_p = _trunk(x_prompt.reshape(batch * seq, d_model), tables, zero_conv, zero_h,
                                     P, attend_prompt, lru_prompt, bm, 512, 1024)

    pos_s = jnp.broadcast_to(past_len + jnp.arange(t_new, dtype=jnp.int32), (nb,))
    tables_s = _rope_tables(pos_s, dqk)

    def attend_sample(l, q, k, v, lambda_init):
        return _decode_attention(q, k, v, cache_k, cache_v, page_table, P["lam"], P["subln_g"], l,
                                 lambda_init)

    def lru_sample(l, xr, gr, conv_t, h0):
        out, ht, buf = _sample_lru(xr, gr, conv_t, h0, lw, l)
        return out, ht, buf.swapaxes(0, 1)

    conv_t = state_conv.swapaxes(1, 2)
    y_s, k_s, v_s, h_s, c_s = _trunk(x_sample.reshape(nb, d_model), tables_s, conv_t, state_h,
                                     P, attend_sample, lru_sample, nb, nb, 2048)

    return (y_p.reshape(batch, seq, d_model),
            y_s.reshape(nb, t_new, d_model),
            k_p.reshape(depth, batch, seq, n_maps, dqk),
            v_p.reshape(depth, batch, seq, N_HEADS, dv),
            h_p.reshape(depth, batch, d_lru),
            c_p,
            k_s.reshape(depth, nb, t_new, n_maps, dqk),
            v_s.reshape(depth, nb, t_new, N_HEADS, dv),
            h_s,
            c_s)
```

```python
import functools
import math

import jax
import jax.numpy as jnp
from jax import lax
from jax.experimental import pallas as pl
from jax.experimental.pallas import tpu as pltpu

F32 = jnp.float32
BF16 = jnp.bfloat16

LANES_V7X = 128
VMEM_LIMIT_BYTES_V7X = 56 * 1024 * 1024

N_HEADS = 8
ROPE_THETA = 500000.0
ROT_FRACTION = 4
LRU_C = 8.0
EPS = 1e-6
SUBLN_EPS = 1e-5
NEG_INF = -1e30
PAGES_PER_STEP = 8


def _params(*sem):
    return pltpu.CompilerParams(dimension_semantics=sem, vmem_limit_bytes=VMEM_LIMIT_BYTES_V7X)


def _rms(y, g, eps):
    return y * lax.rsqrt(jnp.mean(y * y, axis=-1, keepdims=True) + eps) * g


def _rmsnorm_kernel(x_ref, g_ref, o_ref):
    o_ref[...] = _rms(x_ref[...], g_ref[...], EPS).astype(o_ref.dtype)


def _rmsnorm(x, g_all, layer, bm):
    m, d = x.shape
    return pl.pallas_call(
        _rmsnorm_kernel,
        grid=(m // bm,),
        in_specs=[pl.BlockSpec((bm, d), lambda i: (i, 0)),
                  pl.BlockSpec((None, 1, d), lambda i: (layer, 0, 0))],
        out_specs=pl.BlockSpec((bm, d), lambda i: (i, 0)),
        out_shape=jax.ShapeDtypeStruct((m, d), BF16),
        compiler_params=_params("arbitrary"),
        name="rmsnorm",
    )(x, g_all)


def _in_proj_qk_kernel(xn_ref, w_ref, c_ref, s1_ref, s2_ref, q_ref, k_ref, *, q_scale):
    j = pl.program_id(1)
    z = jnp.dot(xn_ref[...], w_ref[...], preferred_element_type=F32)

    def rope_store(dst_ref, mul):
        c, s1, s2 = c_ref[...], s1_ref[...], s2_ref[...]
        for t in range(dst_ref.shape[1] // LANES_V7X):
            sl = slice(t * LANES_V7X, (t + 1) * LANES_V7X)
            zt = z[:, sl]
            r = (zt * c + pltpu.roll(zt, LANES_V7X - 8, 1) * s1) + pltpu.roll(zt, 8, 1) * s2
            dst_ref[:, sl] = (r * mul).astype(dst_ref.dtype)

    @pl.when(j == 0)
    def _():
        rope_store(q_ref, q_scale)

    @pl.when(j == 1)
    def _():
        rope_store(k_ref, 1.0)


def _in_proj_rest_kernel(xn_ref, w_ref, *out_refs):
    j = pl.program_id(1)
    z = jnp.dot(xn_ref[...], w_ref[...], preferred_element_type=F32)
    for idx, o_ref in enumerate(out_refs):
        @pl.when(j == idx)
        def _(o_ref=o_ref):
            o_ref[...] = z


def _in_proj(xn, w_all, layer, tables, bm, q_scale):
    m, d = xn.shape
    n_sec = 5
    width = w_all.shape[2] // n_sec
    c, s1, s2 = tables
    n_tbl = c.shape[0] // bm
    tbl_spec = pl.BlockSpec((bm, LANES_V7X), lambda i, j: (i % n_tbl, 0))
    x_spec = pl.BlockSpec((bm, d), lambda i, j: (i, 0))
    out_spec = pl.BlockSpec((bm, width), lambda i, j: (i, 0))
    f32_out = jax.ShapeDtypeStruct((m, width), F32)
    q, k = pl.pallas_call(
        functools.partial(_in_proj_qk_kernel, q_scale=q_scale),
        grid=(m // bm, 2),
        in_specs=[x_spec, pl.BlockSpec((None, d, width), lambda i, j: (layer, 0, j)),
                  tbl_spec, tbl_spec, tbl_spec],
        out_specs=[out_spec] * 2,
        out_shape=[jax.ShapeDtypeStruct((m, width), BF16), f32_out],
        compiler_params=_params("arbitrary", "arbitrary"),
        name="in_proj_qk",
    )(xn, w_all, c, s1, s2)
    v, xr, gr = pl.pallas_call(
        _in_proj_rest_kernel,
        grid=(m // bm, n_sec - 2),
        in_specs=[x_spec, pl.BlockSpec((None, d, width), lambda i, j: (layer, 0, j + 2))],
        out_specs=[out_spec] * (n_sec - 2),
        out_shape=[f32_out] * (n_sec - 2),
        compiler_params=_params("arbitrary", "arbitrary"),
        name="in_proj_rest",
    )(xn, w_all)
    return q, k, v, xr, gr


def _diff_lambda(lam_ref, lambda_init):
    lv = lam_ref[...]
    l1 = jnp.sum(lv[0:1] * lv[1:2], axis=1, keepdims=True)
    l2 = jnp.sum(lv[2:3] * lv[3:4], axis=1, keepdims=True)
    return jnp.exp(l1) - jnp.exp(l2) + lambda_init


def _diff_combine(o0, o1, lam, g, lambda_init):
    d = o0 - lam * o1
    d = d * lax.rsqrt(jnp.mean(d * d, axis=-1, keepdims=True) + SUBLN_EPS)
    return d * g * (1.0 - lambda_init)


def _reduce_rows(x, op, reduce):
    rows = x.shape[0]
    while rows > 8 and rows % 16 == 0:
        rows //= 2
        x = op(x[:rows], x[rows:])
    return reduce(x, axis=0, keepdims=True)


def _attn_kernel(q_ref, k_ref, v_ref, lam_ref, g_ref, o_ref,
                 kb_ref, vt_ref, qt_ref, m_ref, l_ref, acc_ref, *, bq, cw, dqk, lambda_init):
    qi = pl.program_id(2)
    nk = kb_ref.shape[0]

    @pl.when(qi == 0)
    def _():
        for j in range(nk):
            kb_ref[j] = k_ref[j * bq:(j + 1) * bq, :].astype(BF16)
            vt_ref[j] = v_ref[j * bq:(j + 1) * bq, :].T.astype(BF16)

    qt = q_ref[...].astype(F32).T
    row = lax.broadcasted_iota(jnp.int32, qt.shape, 0)
    qt_ref[:, :bq] = jnp.where(row < dqk, qt, 0.0).astype(BF16)
    qt_ref[:, bq:] = jnp.where(row >= dqk, qt, 0.0).astype(BF16)

    m_ref[...] = jnp.full(m_ref.shape, NEG_INF, F32)
    l_ref[...] = jnp.zeros(l_ref.shape, F32)
    acc_ref[...] = jnp.zeros(acc_ref.shape, F32)

    def block(kj, masked):
        for c in range(2 * bq // cw):
            cs = slice(c * cw, (c + 1) * cw)
            q_lo = (c * cw) % bq
            klen = min(bq, q_lo + cw) if masked else bq
            s = jnp.dot(kb_ref[kj, :klen, :], qt_ref[:, cs], preferred_element_type=F32)
            if masked:
                key = lax.broadcasted_iota(jnp.int32, s.shape, 0)
                qry = (lax.broadcasted_iota(jnp.int32, s.shape, 1) + c * cw) & (bq - 1)
                s = jnp.where(key <= qry, s, NEG_INF)
            m_prev = m_ref[:, cs]
            m_new = jnp.maximum(m_prev, _reduce_rows(s, jnp.maximum, jnp.max))
            alpha = jnp.exp(m_prev - m_new)
            p = jnp.exp(s - m_new)
            l_ref[:, cs] = alpha * l_ref[:, cs] + _reduce_rows(p, jnp.add, jnp.sum)
            pv = jnp.dot(vt_ref[kj, :, :klen], p.astype(BF16), preferred_element_type=F32)
            acc_ref[:, cs] = alpha * acc_ref[:, cs] + pv
            m_ref[:, cs] = m_new

    def body(kj, carry):
        block(kj, False)
        return carry

    lax.fori_loop(0, qi, body, 0)
    block(qi, True)

    o = (acc_ref[...] / l_ref[...]).T
    lam = _diff_lambda(lam_ref, lambda_init)
    o_ref[...] = _diff_combine(o[:bq], o[bq:], lam, g_ref[...], lambda_init).astype(o_ref.dtype)


def _prompt_attention(q, k, v, lam_all, g_all, layer, batch, seq, bq, lambda_init):
    m, width = q.shape
    dv = width // N_HEADS
    dqk = dv // 2
    nq = seq // bq
    cw = 2 * bq
    assert seq % bq == 0 and (2 * bq) % cw == 0
    return pl.pallas_call(
        functools.partial(_attn_kernel, bq=bq, cw=cw, dqk=dqk, lambda_init=lambda_init),
        grid=(batch, N_HEADS, nq),
        in_specs=[pl.BlockSpec((bq, dv), lambda b, h, i: (b * nq + i, h)),
                  pl.BlockSpec((seq, dv), lambda b, h, i: (b, h)),
                  pl.BlockSpec((seq, dv), lambda b, h, i: (b, h)),
                  pl.BlockSpec((None, 4, dqk), lambda b, h, i: (layer, 0, 0)),
                  pl.BlockSpec((None, 1, dv), lambda b, h, i: (layer, 0, 0))],
        out_specs=pl.BlockSpec((bq, dv), lambda b, h, i: (b * nq + i, h)),
        out_shape=jax.ShapeDtypeStruct((m, width), BF16),
        scratch_shapes=[pltpu.VMEM((nq, bq, dv), BF16),
                        pltpu.VMEM((nq, dv, bq), BF16),
                        pltpu.VMEM((dv, 2 * bq), BF16),
                        pltpu.VMEM((1, 2 * bq), F32),
                        pltpu.VMEM((1, 2 * bq), F32),
                        pltpu.VMEM((dv, 2 * bq), F32)],
        compiler_params=_params("arbitrary", "arbitrary", "arbitrary"),
        name="prompt_attention",
    )(q, k, v, lam_all, g_all)


def _decode_attn_kernel(pt_ref, q_ref, kn_ref, vn_ref, lam_ref, g_ref, *rest, lambda_init):
    del pt_ref
    k_refs = rest[:PAGES_PER_STEP]
    v_refs = rest[PAGES_PER_STEP:2 * PAGES_PER_STEP]
    o_ref, qe_ref, qo_ref, m_ref, l_ref, acc_ref = rest[2 * PAGES_PER_STEP:]
    p_idx = pl.program_id(1)
    n_maps = q_ref.shape[1]
    n_heads = n_maps // 2
    rows = v_refs[0].shape[0]

    @pl.when(p_idx == 0)
    def _():
        q = q_ref[0].astype(F32)
        even = (lax.broadcasted_iota(jnp.int32, q.shape, 0) & 1) == 0
        qe_ref[...] = jnp.where(even, q, 0.0).astype(BF16)
        qo_ref[...] = jnp.where(even, 0.0, q).astype(BF16)
        kn = kn_ref[0].astype(BF16).astype(F32)
        m_ref[...] = jnp.sum(q * kn, axis=1, keepdims=True)
        l_ref[...] = jnp.ones(l_ref.shape, F32)
        vn = vn_ref[0].astype(BF16).astype(F32)
        acc_ref[pl.ds(0, n_heads, stride=2), :] = vn
        acc_ref[pl.ds(1, n_heads, stride=2), :] = vn

    row = lax.broadcasted_iota(jnp.int32, (n_maps, rows), 0)
    col = lax.broadcasted_iota(jnp.int32, (n_maps, rows), 1)
    own_head = (col & (n_heads - 1)) == lax.shift_right_logical(row, 1)
    nt = (((1,), (1,)), ((), ()))
    qe, qo = qe_ref[...], qo_ref[...]
    scores = []
    for kp_ref in k_refs:
        ke = kp_ref[pl.ds(0, rows, stride=2), :].astype(BF16)
        ko = kp_ref[pl.ds(1, rows, stride=2), :].astype(BF16)
        s = (lax.dot_general(qe, ke, nt, preferred_element_type=F32)
             + lax.dot_general(qo, ko, nt, preferred_element_type=F32))
        scores.append(jnp.where(own_head, s, NEG_INF))
    m_prev = m_ref[...]
    m_new = m_prev
    for s in scores:
        m_new = jnp.maximum(m_new, jnp.max(s, axis=1, keepdims=True))
    alpha = jnp.exp(m_prev - m_new)
    l_new = alpha * l_ref[...]
    acc = alpha * acc_ref[...]
    for s, vp_ref in zip(scores, v_refs):
        p = jnp.exp(s - m_new)
        l_new = l_new + jnp.sum(p, axis=1, keepdims=True)
        acc = acc + jnp.dot(p.astype(BF16), vp_ref[...].astype(BF16), preferred_element_type=F32)
    l_ref[...] = l_new
    acc_ref[...] = acc
    m_ref[...] = m_new

    @pl.when(p_idx == pl.num_programs(1) - 1)
    def _():
        acc_ref[...] = acc_ref[...] / l_ref[...]
        o0 = acc_ref[pl.ds(0, n_heads, stride=2), :]
        o1 = acc_ref[pl.ds(1, n_heads, stride=2), :]
        lam = _diff_lambda(lam_ref, lambda_init)
        o_ref[0] = _diff_combine(o0, o1, lam, g_ref[...], lambda_init).astype(o_ref.dtype)


def _decode_attention(q, k_new, v_new, cache_k, cache_v, page_table, lam_all, g_all, layer,
                      lambda_init):
    nb, width = q.shape
    depth, n_pool, page, n_maps, dqk = cache_k.shape
    n_heads, dv = cache_v.shape[-2:]
    n_pages = page_table.shape[1]
    assert n_pages % PAGES_PER_STEP == 0 and n_maps == 2 * n_heads and (n_heads & (n_heads - 1)) == 0
    ck = cache_k.reshape(depth, n_pool, page * n_maps, dqk)
    cv = cache_v.reshape(depth, n_pool, page * n_heads, dv)

    def page_spec(rows, cols):
        return [pl.BlockSpec((None, None, rows, cols),
                             functools.partial(lambda b, p, pt, i: (layer, pt[b, p * PAGES_PER_STEP + i], 0, 0), i=i))
                for i in range(PAGES_PER_STEP)]

    map_spec = pl.BlockSpec((1, n_maps, dqk), lambda b, p, pt: (b, 0, 0))
    head_spec = pl.BlockSpec((1, n_heads, dv), lambda b, p, pt: (b, 0, 0))
    grid_spec = pltpu.PrefetchScalarGridSpec(
        num_scalar_prefetch=1,
        grid=(nb, n_pages // PAGES_PER_STEP),
        in_specs=[map_spec, map_spec, head_spec,
                  pl.BlockSpec((None, 4, dqk), lambda b, p, pt: (layer, 0, 0)),
                  pl.BlockSpec((None, 1, dv), lambda b, p, pt: (layer, 0, 0))]
                 + page_spec(page * n_maps, dqk) + page_spec(page * n_heads, dv),
        out_specs=head_spec,
        scratch_shapes=[pltpu.VMEM((n_maps, dqk), BF16),
                        pltpu.VMEM((n_maps, dqk), BF16),
                        pltpu.VMEM((n_maps, 1), F32),
                        pltpu.VMEM((n_maps, 1), F32),
                        pltpu.VMEM((n_maps, dv), F32)],
    )
    out = pl.pallas_call(
        functools.partial(_decode_attn_kernel, lambda_init=lambda_init),
        grid_spec=grid_spec,
        out_shape=jax.ShapeDtypeStruct((nb, n_heads, dv), BF16),
        compiler_params=_params("arbitrary", "arbitrary"),
        name="decode_attention",
    )(page_table, q.reshape(nb, n_maps, dqk), k_new.reshape(nb, n_maps, dqk),
      v_new.reshape(nb, n_heads, dv), lam_all, g_all,
      *([ck] * PAGES_PER_STEP), *([cv] * PAGES_PER_STEP))
    return out.reshape(nb, width)


def _lru_gates(xc, wg_ref, ba_ref, bx_ref, sp, store):
    xcb = xc.astype(BF16)
    bw = wg_ref.shape[1]
    for hb in range(wg_ref.shape[0]):
        sl = slice(hb * bw, (hb + 1) * bw)
        y = jnp.dot(xcb[:, sl], wg_ref[hb], preferred_element_type=F32)
        r = jax.nn.sigmoid(y[:, :bw] + ba_ref[:, sl])
        i = jax.nn.sigmoid(y[:, bw:] + bx_ref[:, sl])
        log_a = -LRU_C * r * sp[:, sl]
        a = jnp.exp(log_a)
        u = jnp.sqrt(-jnp.tanh(log_a) * (a * a + 1.0)) * (i * xc[:, sl])
        store(sl, a, u)


def _lru_kernel(xr_ref, gr_ref, conv0_ref, h0_ref, cw_ref, cb_ref, wg_ref, ba_ref, bx_ref, lam_ref,
                out_ref, ht_ref, cv_ref, xp_ref, a_ref, u_ref, h_ref, *, ts):
    ti = pl.program_id(1)
    taps = cw_ref.shape[0]
    pad = 8
    hist = pad - (taps - 1)

    @pl.when(ti == 0)
    def _():
        xp_ref[hist:pad, :] = conv0_ref[0]
        h_ref[...] = h0_ref[0]

    xp_ref[pad:pad + ts, :] = xr_ref[...]
    cw = cw_ref[...]
    conv = cw[0:1] * xp_ref[hist:hist + ts, :]
    for j in range(1, taps):
        conv = conv + cw[j:j + 1] * xp_ref[hist + j:hist + j + ts, :]
    xc = cb_ref[...] + conv
    sp = jax.nn.softplus(-lam_ref[...])

    def store(sl, a, u):
        a_ref[:, sl] = a
        u_ref[:, sl] = u

    _lru_gates(xc, wg_ref, ba_ref, bx_ref, sp, store)

    def step(t, h):
        h = a_ref[pl.ds(t, 1), :] * h + u_ref[pl.ds(t, 1), :]
        u_ref[pl.ds(t, 1), :] = h
        return h

    h_last = lax.fori_loop(0, ts, step, h_ref[...], unroll=8)
    h_ref[...] = h_last
    ht_ref[0] = h_last
    tail = xp_ref[hist + ts:pad + ts, :]
    cv_ref[0] = tail
    xp_ref[hist:pad, :] = tail
    out_ref[...] = (u_ref[...] * jax.nn.gelu(gr_ref[...])).astype(out_ref.dtype)


def _lru_weight_specs(layer, taps, d, nblk, bw, nargs):
    idx = {1: lambda b: (layer, 0, 0), 2: lambda b, t: (layer, 0, 0)}[nargs]
    idx4 = {1: lambda b: (layer, 0, 0, 0), 2: lambda b, t: (layer, 0, 0, 0)}[nargs]
    vec = pl.BlockSpec((None, 1, d), idx)
    return [pl.BlockSpec((None, taps, d), idx), vec,
            pl.BlockSpec((None, nblk, bw, 2 * bw), idx4), vec, vec, vec]


def _prompt_lru(xr, gr, conv0, h0, lw, layer, batch, seq, ts):
    m, d = xr.shape
    nt = seq // ts
    taps = lw[0].shape[1]
    nblk, bw = lw[2].shape[1], lw[2].shape[2]
    tile = pl.BlockSpec((ts, d), lambda b, t: (b * nt + t, 0))
    return pl.pallas_call(
        functools.partial(_lru_kernel, ts=ts),
        grid=(batch, nt),
        in_specs=[tile, tile,
                  pl.BlockSpec((1, taps - 1, d), lambda b, t: (b, 0, 0)),
                  pl.BlockSpec((1, 1, d), lambda b, t: (b, 0, 0))]
                 + _lru_weight_specs(layer, taps, d, nblk, bw, 2),
        out_specs=[tile,
                   pl.BlockSpec((1, 1, d), lambda b, t: (b, 0, 0)),
                   pl.BlockSpec((1, taps - 1, d), lambda b, t: (b, 0, 0))],
        out_shape=[jax.ShapeDtypeStruct((m, d), BF16),
                   jax.ShapeDtypeStruct((batch, 1, d), F32),
                   jax.ShapeDtypeStruct((batch, taps - 1, d), F32)],
        scratch_shapes=[pltpu.VMEM((ts + 8, d), F32),
                        pltpu.VMEM((ts, d), F32),
                        pltpu.VMEM((ts, d), F32),
                        pltpu.VMEM((1, d), F32)],
        compiler_params=_params("arbitrary", "arbitrary"),
        name="prompt_lru",
    )(xr, gr, conv0, h0.reshape(batch, 1, d), *lw)


def _sample_lru_kernel(xr_ref, gr_ref, conv_ref, h0_ref, cw_ref, cb_ref, wg_ref, ba_ref, bx_ref,
                       lam_ref, out_ref, ht_ref, cv_ref):
    taps = cw_ref.shape[0]
    xr = xr_ref[...]
    cw = cw_ref[...]
    conv = cw[0:1] * conv_ref[0]
    for j in range(1, taps - 1):
        conv = conv + cw[j:j + 1] * conv_ref[j]
    conv = conv + cw[taps - 1:taps] * xr
    xc = cb_ref[...] + conv
    sp = jax.nn.softplus(-lam_ref[...])
    h0 = h0_ref[...]
    gate = jax.nn.gelu(gr_ref[...])

    def store(sl, a, u):
        h = a * h0[:, sl] + u
        ht_ref[:, sl] = h
        out_ref[:, sl] = (h * gate[:, sl]).astype(out_ref.dtype)

    _lru_gates(xc, wg_ref, ba_ref, bx_ref, sp, store)
    for j in range(taps - 2):
        cv_ref[j] = conv_ref[j + 1]
    cv_ref[taps - 2] = xr


def _sample_lru(xr, gr, conv_t, h0, lw, layer):
    nb, d = xr.shape
    taps = lw[0].shape[1]
    nblk, bw = lw[2].shape[1], lw[2].shape[2]
    row = pl.BlockSpec((nb, d), lambda b: (0, 0))
    hist = pl.BlockSpec((taps - 1, nb, d), lambda b: (0, 0, 0))
    return pl.pallas_call(
        _sample_lru_kernel,
        grid=(1,),
        in_specs=[row, row, hist, row] + _lru_weight_specs(layer, taps, d, nblk, bw, 1),
        out_specs=[row, row, hist],
        out_shape=[jax.ShapeDtypeStruct((nb, d), BF16),
                   jax.ShapeDtypeStruct((nb, d), F32),
                   jax.ShapeDtypeStruct((taps - 1, nb, d), F32)],
        compiler_params=_params("arbitrary"),
        name="sample_lru",
    )(xr, gr, conv_t, h0, *lw)


def _out_proj_kernel(a_ref, r_ref, w_ref, x_ref, g_ref, xo_ref, xn_ref):
    half = a_ref.shape[1]
    y = jnp.dot(a_ref[...], w_ref[:half, :], preferred_element_type=F32)
    y = y + jnp.dot(r_ref[...], w_ref[half:, :], preferred_element_type=F32)
    x = x_ref[...] + y
    xo_ref[...] = x
    xn_ref[...] = _rms(x, g_ref[...], EPS).astype(xn_ref.dtype)


def _out_proj(attn, lru, w_all, x, g_all, layer, bm):
    m, half = attn.shape
    d = x.shape[1]
    return pl.pallas_call(
        _out_proj_kernel,
        grid=(m // bm,),
        in_specs=[pl.BlockSpec((bm, half), lambda i: (i, 0)),
                  pl.BlockSpec((bm, half), lambda i: (i, 0)),
                  pl.BlockSpec((None, 2 * half, d), lambda i: (layer, 0, 0)),
                  pl.BlockSpec((bm, d), lambda i: (i, 0)),
                  pl.BlockSpec((None, 1, d), lambda i: (layer, 0, 0))],
        out_specs=[pl.BlockSpec((bm, d), lambda i: (i, 0)),
                   pl.BlockSpec((bm, d), lambda i: (i, 0))],
        out_shape=[jax.ShapeDtypeStruct((m, d), F32), jax.ShapeDtypeStruct((m, d), BF16)],
        compiler_params=_params("arbitrary"),
        name="out_proj",
    )(attn, lru, w_all, x, g_all)


def _mlp_up_kernel(xn_ref, w_ref, h_ref):
    y = jnp.dot(xn_ref[...], w_ref[...], preferred_element_type=F32)
    h_ref[...] = jnp.square(jnp.maximum(y, 0.0)).astype(h_ref.dtype)


def _mlp_up(xn, w_all, layer, bm, bn):
    m, d = xn.shape
    f = w_all.shape[2]
    return pl.pallas_call(
        _mlp_up_kernel,
        grid=(m // bm, f // bn),
        in_specs=[pl.BlockSpec((bm, d), lambda i, j: (i, 0)),
                  pl.BlockSpec((None, d, bn), lambda i, j: (layer, 0, j))],
        out_specs=pl.BlockSpec((bm, bn), lambda i, j: (i, j)),
        out_shape=jax.ShapeDtypeStruct((m, f), BF16),
        compiler_params=_params("arbitrary", "arbitrary"),
        name="mlp_up",
    )(xn, w_all)


def _mlp_down_kernel(h_ref, w_ref, x_ref, g_ref, xo_ref, xn_ref):
    kk = pl.program_id(1)
    y = jnp.dot(h_ref[...], w_ref[...], preferred_element_type=F32)

    @pl.when(kk == 0)
    def _():
        xo_ref[...] = x_ref[...] + y

    @pl.when(kk > 0)
    def _():
        xo_ref[...] += y

    @pl.when(kk == pl.num_programs(1) - 1)
    def _():
        xn_ref[...] = _rms(xo_ref[...], g_ref[...], EPS).astype(xn_ref.dtype)


def _mlp_down(h, w_all, layer, x, g, norm_dtype, bm, bk):
    m, f = h.shape
    d = x.shape[1]
    return pl.pallas_call(
        _mlp_down_kernel,
        grid=(m // bm, f // bk),
        in_specs=[pl.BlockSpec((bm, bk), lambda i, k: (i, k)),
                  pl.BlockSpec((None, bk, d), lambda i, k: (layer, k, 0)),
                  pl.BlockSpec((bm, d), lambda i, k: (i, 0)),
                  pl.BlockSpec((1, d), lambda i, k: (0, 0))],
        out_specs=[pl.BlockSpec((bm, d), lambda i, k: (i, 0)),
                   pl.BlockSpec((bm, d), lambda i, k: (i, 0))],
        out_shape=[jax.ShapeDtypeStruct((m, d), F32), jax.ShapeDtypeStruct((m, d), norm_dtype)],
        compiler_params=_params("arbitrary", "arbitrary"),
        name="mlp_down",
    )(h, w_all, x, g)


def _rope_tables(positions, dqk):
    rot = dqk // ROT_FRACTION
    inv_freq = ROPE_THETA ** (-jnp.arange(0, rot, 2, dtype=F32) / rot)
    ang = positions.astype(F32)[:, None] * inv_freq[None, :]
    cos, sin = jnp.cos(ang), jnp.sin(ang)
    n = positions.shape[0]
    zh = jnp.zeros((n, rot // 2), F32)
    rest0 = jnp.zeros((n, dqk - rot), F32)
    c = jnp.concatenate([cos, cos, jnp.ones((n, dqk - rot), F32)], axis=1)
    s1 = jnp.concatenate([-sin, zh, rest0], axis=1)
    s2 = jnp.concatenate([zh, sin, rest0], axis=1)
    reps = LANES_V7X // dqk
    return tuple(jnp.tile(t, (1, reps)) for t in (c, s1, s2))


def _trunk(x, tables, conv0, h0, P, attend, lru, bm, bm_down, bk_down):
    depth = P["w_in"].shape[0]
    dqk = P["lam"].shape[2]
    q_scale = dqk ** -0.5
    xn = _rmsnorm(x, P["norm_mix_g"], 0, bm)
    ks, vs, hts, bufs = [], [], [], []
    y = None
    for l in range(depth):
        lambda_init = 0.8 - 0.6 * math.exp(-0.3 * l)
        q, k, v, xr, gr = _in_proj(xn, P["w_in"], l, tables, bm, q_scale)
        attn = attend(l, q, k, v, lambda_init)
        lru_out, ht, buf = lru(l, xr, gr, conv0[l], h0[l])
        x, xn = _out_proj(attn, lru_out, P["w_out"], x, P["norm_mlp_g"], l, bm_down)
        hmid = _mlp_up(xn, P["w_up"], l, bm, 1024)
        last = l == depth - 1
        g_next = P["final_norm_g"] if last else P["norm_mix_g"][l + 1]
        x, xn = _mlp_down(hmid, P["w_down"], l, x, g_next, F32 if last else BF16, bm_down, bk_down)
        y = xn
        ks.append(k)
        vs.append(v)
        hts.append(ht)
        bufs.append(buf)
    return y, jnp.stack(ks), jnp.stack(vs), jnp.stack(hts), jnp.stack(bufs)


def kernel(x_prompt, x_sample, cache_k, cache_v, state_h, state_conv, page_table, norm_mix_g, w_in, lambda_q1, lambda_k1, lambda_q2, lambda_k2, subln_g, conv_w, conv_b, w_gate_a, b_gate_a, w_gate_x, b_gate_x, lru_lambda, w_out, norm_mlp_g, w_up, w_down, final_norm_g):
    batch, seq, d_model = x_prompt.shape
    nb, t_new, _ = x_sample.shape
    depth, _, page, n_maps, dqk = cache_k.shape
    dv = cache_v.shape[-1]
    d_lru = state_h.shape[-1]
    taps = conv_w.shape[1]
    assert t_new == 1, "the sample kernels handle one new token per sequence"
    past_len = page_table.shape[1] * page

    vec = lambda a: a.reshape(depth, 1, a.shape[-1])
    P = {
        "w_in": w_in.astype(BF16), "w_out": w_out.astype(BF16),
        "w_up": w_up.astype(BF16), "w_down": w_down.astype(BF16),
        "norm_mix_g": vec(norm_mix_g), "norm_mlp_g": vec(norm_mlp_g),
        "final_norm_g": final_norm_g.reshape(1, d_model),
        "lam": jnp.stack([lambda_q1, lambda_k1, lambda_q2, lambda_k2], axis=1),
        "subln_g": vec(subln_g),
    }
    lw = (conv_w, vec(conv_b), jnp.concatenate([w_gate_a, w_gate_x], axis=-1).astype(BF16),
          vec(b_gate_a), vec(b_gate_x), vec(lru_lambda))

    bm = 1024
    tables = _rope_tables(jnp.arange(seq, dtype=jnp.int32), dqk)

    def attend_prompt(l, q, k, v, lambda_init):
        return _prompt_attention(q, k, v, P["lam"], P["subln_g"], l, batch, seq, 512, lambda_init)

    def lru_prompt(l, xr, gr, conv0, h0):
        return _prompt_lru(xr, gr, conv0, h0, lw, l, batch, seq, 512)

    zero_conv = jnp.zeros((depth, batch, taps - 1, d_lru), F32)
    zero_h = jnp.zeros((depth, batch, d_lru), F32)
    y_p, k_p, v_p, h_p, c_p = _trunk(x_prompt.reshape(batch * seq, d_model), tables, zero_conv, zero_h,
                                     P, attend_prompt, lru_prompt, bm, 512, 2048)

    pos_s = jnp.broadcast_to(past_len + jnp.arange(t_new, dtype=jnp.int32), (nb,))
    tables_s = _rope_tables(pos_s, dqk)

    def attend_sample(l, q, k, v, lambda_init):
        return _decode_attention(q, k, v, cache_k, cache_v, page_table, P["lam"], P["subln_g"], l,
                                 lambda_init)

    def lru_sample(l, xr, gr, conv_t, h0):
        out, ht, buf = _sample_lru(xr, gr, conv_t, h0, lw, l)
        return out, ht, buf.swapaxes(0, 1)

    conv_t = state_conv.swapaxes(1, 2)
    y_s, k_s, v_s, h_s, c_s = _trunk(x_sample.reshape(nb, d_model), tables_s, conv_t, state_h,
                                     P, attend_sample, lru_sample, nb, nb, 2048)

    return (y_p.reshape(batch, seq, d_model),
            y_s.reshape(nb, t_new, d_model),
            k_p.reshape(depth, batch, seq, n_maps, dqk),
            v_p.reshape(depth, batch, seq, N_HEADS, dv),
            h_p.reshape(depth, batch, d_lru),
            c_p,
            k_s.reshape(depth, nb, t_new, n_maps, dqk),
            v_s.reshape(depth, nb, t_new, N_HEADS, dv),
            h_s,
            c_s)
```

```python
import functools
import math

import jax
import jax.numpy as jnp
from jax import lax
from jax.experimental import pallas as pl
from jax.experimental.pallas import tpu as pltpu

F32 = jnp.float32
BF16 = jnp.bfloat16

LANES_V7X = 128
VMEM_LIMIT_BYTES_V7X = 56 * 1024 * 1024

N_HEADS = 8
ROPE_THETA = 500000.0
ROT_FRACTION = 4
LRU_C = 8.0
EPS = 1e-6
SUBLN_EPS = 1e-5
NEG_INF = -1e30
PAGES_PER_STEP = 8


def _params(*sem):
    return pltpu.CompilerParams(dimension_semantics=sem, vmem_limit_bytes=VMEM_LIMIT_BYTES_V7X)


def _rms(y, g, eps):
    return y * lax.rsqrt(jnp.mean(y * y, axis=-1, keepdims=True) + eps) * g


def _rmsnorm_kernel(x_ref, g_ref, o_ref):
    o_ref[...] = _rms(x_ref[...], g_ref[...], EPS).astype(o_ref.dtype)


def _rmsnorm(x, g_all, layer, bm):
    m, d = x.shape
    return pl.pallas_call(
        _rmsnorm_kernel,
        grid=(m // bm,),
        in_specs=[pl.BlockSpec((bm, d), lambda i: (i, 0)),
                  pl.BlockSpec((None, 1, d), lambda i: (layer, 0, 0))],
        out_specs=pl.BlockSpec((bm, d), lambda i: (i, 0)),
        out_shape=jax.ShapeDtypeStruct((m, d), BF16),
        compiler_params=_params("arbitrary"),
        name="rmsnorm",
    )(x, g_all)


def _in_proj_qk_kernel(xn_ref, w_ref, c_ref, s1_ref, s2_ref, q_ref, k_ref, *, q_scale):
    j = pl.program_id(1)
    z = jnp.dot(xn_ref[...], w_ref[...], preferred_element_type=F32)

    def rope_store(dst_ref, mul):
        c, s1, s2 = c_ref[...], s1_ref[...], s2_ref[...]
        for t in range(dst_ref.shape[1] // LANES_V7X):
            sl = slice(t * LANES_V7X, (t + 1) * LANES_V7X)
            zt = z[:, sl]
            r = (zt * c + pltpu.roll(zt, LANES_V7X - 8, 1) * s1) + pltpu.roll(zt, 8, 1) * s2
            dst_ref[:, sl] = (r * mul).astype(dst_ref.dtype)

    @pl.when(j == 0)
    def _():
        rope_store(q_ref, q_scale)

    @pl.when(j == 1)
    def _():
        rope_store(k_ref, 1.0)


def _in_proj_rest_kernel(xn_ref, w_ref, *out_refs):
    j = pl.program_id(1)
    z = jnp.dot(xn_ref[...], w_ref[...], preferred_element_type=F32)
    for idx, o_ref in enumerate(out_refs):
        @pl.when(j == idx)
        def _(o_ref=o_ref):
            o_ref[...] = z


def _in_proj(xn, w_all, layer, tables, bm, q_scale):
    m, d = xn.shape
    n_sec = 5
    width = w_all.shape[2] // n_sec
    c, s1, s2 = tables
    n_tbl = c.shape[0] // bm
    tbl_spec = pl.BlockSpec((bm, LANES_V7X), lambda i, j: (i % n_tbl, 0))
    x_spec = pl.BlockSpec((bm, d), lambda i, j: (i, 0))
    out_spec = pl.BlockSpec((bm, width), lambda i, j: (i, 0))
    f32_out = jax.ShapeDtypeStruct((m, width), F32)
    q, k = pl.pallas_call(
        functools.partial(_in_proj_qk_kernel, q_scale=q_scale),
        grid=(m // bm, 2),
        in_specs=[x_spec, pl.BlockSpec((None, d, width), lambda i, j: (layer, 0, j)),
                  tbl_spec, tbl_spec, tbl_spec],
        out_specs=[out_spec] * 2,
        out_shape=[jax.ShapeDtypeStruct((m, width), BF16), f32_out],
        compiler_params=_params("arbitrary", "arbitrary"),
        name="in_proj_qk",
    )(xn, w_all, c, s1, s2)
    v, xr, gr = pl.pallas_call(
        _in_proj_rest_kernel,
        grid=(m // bm, n_sec - 2),
        in_specs=[x_spec, pl.BlockSpec((None, d, width), lambda i, j: (layer, 0, j + 2))],
        out_specs=[out_spec] * (n_sec - 2),
        out_shape=[f32_out] * (n_sec - 2),
        compiler_params=_params("arbitrary", "arbitrary"),
        name="in_proj_rest",
    )(xn, w_all)
    return q, k, v, xr, gr


def _diff_lambda(lam_ref, lambda_init):
    lv = lam_ref[...]
    l1 = jnp.sum(lv[0:1] * lv[1:2], axis=1, keepdims=True)
    l2 = jnp.sum(lv[2:3] * lv[3:4], axis=1, keepdims=True)
    return jnp.exp(l1) - jnp.exp(l2) + lambda_init


def _diff_combine(o0, o1, lam, g, lambda_init):
    d = o0 - lam * o1
    d = d * lax.rsqrt(jnp.mean(d * d, axis=-1, keepdims=True) + SUBLN_EPS)
    return d * g * (1.0 - lambda_init)


def _reduce_rows(x, op, reduce):
    rows = x.shape[0]
    while rows > 8 and rows % 16 == 0:
        rows //= 2
        x = op(x[:rows], x[rows:])
    return reduce(x, axis=0, keepdims=True)


def _attn_kernel(q_ref, k_ref, v_ref, lam_ref, g_ref, o_ref,
                 kb_ref, vt_ref, qt_ref, s0_ref, s1_ref, m_ref, l_ref, acc_ref,
                 *, bq, cw, dqk, lambda_init):
    qi = pl.program_id(2)
    nk = kb_ref.shape[0]

    @pl.when(qi == 0)
    def _():
        for j in range(nk):
            kb_ref[j] = k_ref[j * bq:(j + 1) * bq, :].astype(BF16)
            vt_ref[j] = v_ref[j * bq:(j + 1) * bq, :].T.astype(BF16)

    qt = q_ref[...].astype(F32).T
    row = lax.broadcasted_iota(jnp.int32, qt.shape, 0)
    qt_ref[:, :bq] = jnp.where(row < dqk, qt, 0.0).astype(BF16)
    qt_ref[:, bq:] = jnp.where(row >= dqk, qt, 0.0).astype(BF16)

    m_ref[...] = jnp.full(m_ref.shape, NEG_INF, F32)
    l_ref[...] = jnp.zeros(l_ref.shape, F32)
    acc_ref[...] = jnp.zeros(acc_ref.shape, F32)

    def scores(kj, s_ref):
        s_ref[...] = jnp.dot(kb_ref[kj], qt_ref[...], preferred_element_type=F32)

    def softmax_pv(kj, s_ref, masked):
        for c in range(2 * bq // cw):
            cs = slice(c * cw, (c + 1) * cw)
            q_lo = (c * cw) % bq
            klen = min(bq, q_lo + cw) if masked else bq
            s = s_ref[:klen, cs]
            if masked:
                key = lax.broadcasted_iota(jnp.int32, s.shape, 0)
                qry = (lax.broadcasted_iota(jnp.int32, s.shape, 1) + c * cw) & (bq - 1)
                s = jnp.where(key <= qry, s, NEG_INF)
            m_prev = m_ref[:, cs]
            m_new = jnp.maximum(m_prev, _reduce_rows(s, jnp.maximum, jnp.max))
            alpha = jnp.exp(m_prev - m_new)
            p = jnp.exp(s - m_new)
            l_ref[:, cs] = alpha * l_ref[:, cs] + _reduce_rows(p, jnp.add, jnp.sum)
            pv = jnp.dot(vt_ref[kj, :, :klen], p.astype(BF16), preferred_element_type=F32)
            acc_ref[:, cs] = alpha * acc_ref[:, cs] + pv
            m_ref[:, cs] = m_new

    def stage(kj, masked):
        for parity, (cur, nxt) in enumerate(((s0_ref, s1_ref), (s1_ref, s0_ref))):
            @pl.when((kj & 1) == parity)
            def _(cur=cur, nxt=nxt):
                if not masked:
                    scores(kj + 1, nxt)
                softmax_pv(kj, cur, masked)

    scores(0, s0_ref)

    def body(kj, carry):
        stage(kj, False)
        return carry

    lax.fori_loop(0, qi, body, 0)
    stage(qi, True)

    o = (acc_ref[...] / l_ref[...]).T
    lam = _diff_lambda(lam_ref, lambda_init)
    o_ref[...] = _diff_combine(o[:bq], o[bq:], lam, g_ref[...], lambda_init).astype(o_ref.dtype)


def _prompt_attention(q, k, v, lam_all, g_all, layer, batch, seq, bq, lambda_init):
    m, width = q.shape
    dv = width // N_HEADS
    dqk = dv // 2
    nq = seq // bq
    cw = min(bq, 4 * LANES_V7X)
    assert seq % bq == 0 and bq % cw == 0 and (bq & (bq - 1)) == 0
    return pl.pallas_call(
        functools.partial(_attn_kernel, bq=bq, cw=cw, dqk=dqk, lambda_init=lambda_init),
        grid=(batch, N_HEADS, nq),
        in_specs=[pl.BlockSpec((bq, dv), lambda b, h, i: (b * nq + i, h)),
                  pl.BlockSpec((seq, dv), lambda b, h, i: (b, h)),
                  pl.BlockSpec((seq, dv), lambda b, h, i: (b, h)),
                  pl.BlockSpec((None, 4, dqk), lambda b, h, i: (layer, 0, 0)),
                  pl.BlockSpec((None, 1, dv), lambda b, h, i: (layer, 0, 0))],
        out_specs=pl.BlockSpec((bq, dv), lambda b, h, i: (b * nq + i, h)),
        out_shape=jax.ShapeDtypeStruct((m, width), BF16),
        scratch_shapes=[pltpu.VMEM((nq, bq, dv), BF16),
                        pltpu.VMEM((nq, dv, bq), BF16),
                        pltpu.VMEM((dv, 2 * bq), BF16),
                        pltpu.VMEM((bq, 2 * bq), F32),
                        pltpu.VMEM((bq, 2 * bq), F32),
                        pltpu.VMEM((1, 2 * bq), F32),
                        pltpu.VMEM((1, 2 * bq), F32),
                        pltpu.VMEM((dv, 2 * bq), F32)],
        compiler_params=_params("arbitrary", "arbitrary", "arbitrary"),
        name="prompt_attention",
    )(q, k, v, lam_all, g_all)


def _decode_attn_kernel(pt_ref, q_ref, kn_ref, vn_ref, lam_ref, g_ref, *rest, lambda_init):
    del pt_ref
    k_refs = rest[:PAGES_PER_STEP]
    v_refs = rest[PAGES_PER_STEP:2 * PAGES_PER_STEP]
    o_ref, qm_ref, m_ref, l_ref, acc_ref = rest[2 * PAGES_PER_STEP:]
    p_idx = pl.program_id(1)
    n_maps, dqk, page = k_refs[0].shape
    n_heads = n_maps // 2
    width = n_maps * dqk

    @pl.when(p_idx == 0)
    def _():
        row = lax.broadcasted_iota(jnp.int32, (n_maps, width), 0)
        lane = lax.broadcasted_iota(jnp.int32, (n_maps, width), 1)
        own_map = (lane >= row * dqk) & (lane < (row + 1) * dqk)
        qrow = jnp.broadcast_to(q_ref[0].astype(F32), (n_maps, width))
        qm = jnp.where(own_map, qrow, 0.0)
        qm_ref[...] = qm.astype(BF16)
        kn = jnp.broadcast_to(kn_ref[0].astype(BF16).astype(F32), (n_maps, width))
        m_ref[...] = jnp.sum(qm * kn, axis=1, keepdims=True)
        l_ref[...] = jnp.ones(l_ref.shape, F32)
        vn = vn_ref[0].astype(BF16).astype(F32)
        acc_ref[pl.ds(0, n_heads, stride=2), :] = vn
        acc_ref[pl.ds(1, n_heads, stride=2), :] = vn

    qm = qm_ref[...]
    scores = []
    for kp_ref in k_refs:
        kt = kp_ref[...].reshape(width, -1).astype(BF16)
        scores.append(jnp.dot(qm, kt, preferred_element_type=F32))
    m_prev = m_ref[...]
    m_new = m_prev
    for s in scores:
        m_new = jnp.maximum(m_new, jnp.max(s, axis=1, keepdims=True))
    alpha = jnp.exp(m_prev - m_new)
    l_new = alpha * l_ref[...]
    acc = alpha * acc_ref[...]
    own_head = lax.shift_right_logical(lax.broadcasted_iota(jnp.int32, acc.shape, 0), 1)
    for s, vp_ref in zip(scores, v_refs):
        p = jnp.exp(s - m_new)
        l_new = l_new + jnp.sum(p, axis=1, keepdims=True)
        pb = p.astype(BF16)
        for h in range(n_heads):
            vh = vp_ref[pl.ds(h, page, stride=n_heads), :].astype(BF16)
            pv = jnp.dot(pb, vh, preferred_element_type=F32)
            acc = acc + jnp.where(own_head == h, pv, 0.0)
    l_ref[...] = l_new
    acc_ref[...] = acc
    m_ref[...] = m_new

    @pl.when(p_idx == pl.num_programs(1) - 1)
    def _():
        acc_ref[...] = acc_ref[...] / l_ref[...]
        o0 = acc_ref[pl.ds(0, n_heads, stride=2), :]
        o1 = acc_ref[pl.ds(1, n_heads, stride=2), :]
        lam = _diff_lambda(lam_ref, lambda_init)
        o_ref[0] = _diff_combine(o0, o1, lam, g_ref[...], lambda_init).astype(o_ref.dtype)


def _decode_attention(q, k_new, v_new, cache_k, cache_v, page_table, lam_all, g_all, layer,
                      lambda_init):
    nb, width = q.shape
    depth, n_pool, page, n_maps, dqk = cache_k.shape
    n_heads, dv = cache_v.shape[-2:]
    n_pages = page_table.shape[1]
    assert n_pages % PAGES_PER_STEP == 0 and n_maps == 2 * n_heads and (n_heads & (n_heads - 1)) == 0
    ckt = cache_k.transpose(0, 1, 3, 4, 2)
    cv = cache_v.reshape(depth, n_pool, page * n_heads, dv)

    def page_spec(*block):
        zeros = (0,) * len(block)
        return [pl.BlockSpec((None, None) + block,
                             functools.partial(lambda b, p, pt, i: (layer, pt[b, p * PAGES_PER_STEP + i]) + zeros, i=i))
                for i in range(PAGES_PER_STEP)]

    row_spec = pl.BlockSpec((1, 1, width), lambda b, p, pt: (b, 0, 0))
    head_spec = pl.BlockSpec((1, n_heads, dv), lambda b, p, pt: (b, 0, 0))
    grid_spec = pltpu.PrefetchScalarGridSpec(
        num_scalar_prefetch=1,
        grid=(nb, n_pages // PAGES_PER_STEP),
        in_specs=[row_spec, row_spec, head_spec,
                  pl.BlockSpec((None, 4, dqk), lambda b, p, pt: (layer, 0, 0)),
                  pl.BlockSpec((None, 1, dv), lambda b, p, pt: (layer, 0, 0))]
                 + page_spec(n_maps, dqk, page) + page_spec(page * n_heads, dv),
        out_specs=head_spec,
        scratch_shapes=[pltpu.VMEM((n_maps, width), BF16),
                        pltpu.VMEM((n_maps, 1), F32),
                        pltpu.VMEM((n_maps, 1), F32),
                        pltpu.VMEM((n_maps, dv), F32)],
    )
    out = pl.pallas_call(
        functools.partial(_decode_attn_kernel, lambda_init=lambda_init),
        grid_spec=grid_spec,
        out_shape=jax.ShapeDtypeStruct((nb, n_heads, dv), BF16),
        compiler_params=_params("arbitrary", "arbitrary"),
        name="decode_attention",
    )(page_table, q.reshape(nb, 1, width), k_new.reshape(nb, 1, width),
      v_new.reshape(nb, n_heads, dv), lam_all, g_all,
      *([ckt] * PAGES_PER_STEP), *([cv] * PAGES_PER_STEP))
    return out.reshape(nb, width)


def _lru_gates(xc, wg_ref, ba_ref, bx_ref, sp, store):
    xcb = xc.astype(BF16)
    bw = wg_ref.shape[1]
    for hb in range(wg_ref.shape[0]):
        sl = slice(hb * bw, (hb + 1) * bw)
        y = jnp.dot(xcb[:, sl], wg_ref[hb], preferred_element_type=F32)
        r = jax.nn.sigmoid(y[:, :bw] + ba_ref[:, sl])
        i = jax.nn.sigmoid(y[:, bw:] + bx_ref[:, sl])
        log_a = -LRU_C * r * sp[:, sl]
        a = jnp.exp(log_a)
        u = jnp.sqrt(-jnp.tanh(log_a) * (a * a + 1.0)) * (i * xc[:, sl])
        store(sl, a, u)


def _lru_kernel(xr_ref, gr_ref, conv0_ref, h0_ref, cw_ref, cb_ref, wg_ref, ba_ref, bx_ref, lam_ref,
                out_ref, ht_ref, cv_ref, xp_ref, a_ref, u_ref, h_ref, *, ts):
    ti = pl.program_id(1)
    taps = cw_ref.shape[0]
    pad = 8
    hist = pad - (taps - 1)

    @pl.when(ti == 0)
    def _():
        xp_ref[hist:pad, :] = conv0_ref[0]
        h_ref[...] = h0_ref[0]

    xp_ref[pad:pad + ts, :] = xr_ref[...]
    cw = cw_ref[...]
    conv = cw[0:1] * xp_ref[hist:hist + ts, :]
    for j in range(1, taps):
        conv = conv + cw[j:j + 1] * xp_ref[hist + j:hist + j + ts, :]
    xc = cb_ref[...] + conv
    sp = jax.nn.softplus(-lam_ref[...])

    def store(sl, a, u):
        a_ref[:, sl] = a
        u_ref[:, sl] = u

    _lru_gates(xc, wg_ref, ba_ref, bx_ref, sp, store)

    def step(t, h):
        h = a_ref[pl.ds(t, 1), :] * h + u_ref[pl.ds(t, 1), :]
        u_ref[pl.ds(t, 1), :] = h
        return h

    h_last = lax.fori_loop(0, ts, step, h_ref[...], unroll=8)
    h_ref[...] = h_last
    ht_ref[0] = h_last
    tail = xp_ref[hist + ts:pad + ts, :]
    cv_ref[0] = tail
    xp_ref[hist:pad, :] = tail
    out_ref[...] = (u_ref[...] * jax.nn.gelu(gr_ref[...])).astype(out_ref.dtype)


def _lru_weight_specs(layer, taps, d, nblk, bw, nargs):
    idx = {1: lambda b: (layer, 0, 0), 2: lambda b, t: (layer, 0, 0)}[nargs]
    idx4 = {1: lambda b: (layer, 0, 0, 0), 2: lambda b, t: (layer, 0, 0, 0)}[nargs]
    vec = pl.BlockSpec((None, 1, d), idx)
    return [pl.BlockSpec((None, taps, d), idx), vec,
            pl.BlockSpec((None, nblk, bw, 2 * bw), idx4), vec, vec, vec]


def _prompt_lru(xr, gr, conv0, h0, lw, layer, batch, seq, ts):
    m, d = xr.shape
    nt = seq // ts
    taps = lw[0].shape[1]
    nblk, bw = lw[2].shape[1], lw[2].shape[2]
    tile = pl.BlockSpec((ts, d), lambda b, t: (b * nt + t, 0))
    return pl.pallas_call(
        functools.partial(_lru_kernel, ts=ts),
        grid=(batch, nt),
        in_specs=[tile, tile,
                  pl.BlockSpec((1, taps - 1, d), lambda b, t: (b, 0, 0)),
                  pl.BlockSpec((1, 1, d), lambda b, t: (b, 0, 0))]
                 + _lru_weight_specs(layer, taps, d, nblk, bw, 2),
        out_specs=[tile,
                   pl.BlockSpec((1, 1, d), lambda b, t: (b, 0, 0)),
                   pl.BlockSpec((1, taps - 1, d), lambda b, t: (b, 0, 0))],
        out_shape=[jax.ShapeDtypeStruct((m, d), BF16),
                   jax.ShapeDtypeStruct((batch, 1, d), F32),
                   jax.ShapeDtypeStruct((batch, taps - 1, d), F32)],
        scratch_shapes=[pltpu.VMEM((ts + 8, d), F32),
                        pltpu.VMEM((ts, d), F32),
                        pltpu.VMEM((ts, d), F32),
                        pltpu.VMEM((1, d), F32)],
        compiler_params=_params("arbitrary", "arbitrary"),
        name="prompt_lru",
    )(xr, gr, conv0, h0.reshape(batch, 1, d), *lw)


def _sample_lru_kernel(xr_ref, gr_ref, conv_ref, h0_ref, cw_ref, cb_ref, wg_ref, ba_ref, bx_ref,
                       lam_ref, out_ref, ht_ref, cv_ref):
    taps = cw_ref.shape[0]
    xr = xr_ref[...]
    cw = cw_ref[...]
    conv = cw[0:1] * conv_ref[0]
    for j in range(1, taps - 1):
        conv = conv + cw[j:j + 1] * conv_ref[j]
    conv = conv + cw[taps - 1:taps] * xr
    xc = cb_ref[...] + conv
    sp = jax.nn.softplus(-lam_ref[...])
    h0 = h0_ref[...]
    gate = jax.nn.gelu(gr_ref[...])

    def store(sl, a, u):
        h = a * h0[:, sl] + u
        ht_ref[:, sl] = h
        out_ref[:, sl] = (h * gate[:, sl]).astype(out_ref.dtype)

    _lru_gates(xc, wg_ref, ba_ref, bx_ref, sp, store)
    for j in range(taps - 2):
        cv_ref[j] = conv_ref[j + 1]
    cv_ref[taps - 2] = xr


def _sample_lru(xr, gr, conv_t, h0, lw, layer):
    nb, d = xr.shape
    taps = lw[0].shape[1]
    nblk, bw = lw[2].shape[1], lw[2].shape[2]
    row = pl.BlockSpec((nb, d), lambda b: (0, 0))
    hist = pl.BlockSpec((taps - 1, nb, d), lambda b: (0, 0, 0))
    return pl.pallas_call(
        _sample_lru_kernel,
        grid=(1,),
        in_specs=[row, row, hist, row] + _lru_weight_specs(layer, taps, d, nblk, bw, 1),
        out_specs=[row, row, hist],
        out_shape=[jax.ShapeDtypeStruct((nb, d), BF16),
                   jax.ShapeDtypeStruct((nb, d), F32),
                   jax.ShapeDtypeStruct((taps - 1, nb, d), F32)],
        compiler_params=_params("arbitrary"),
        name="sample_lru",
    )(xr, gr, conv_t, h0, *lw)


def _out_proj_kernel(a_ref, r_ref, w_ref, x_ref, g_ref, xo_ref, xn_ref):
    half = a_ref.shape[1]
    y = jnp.dot(a_ref[...], w_ref[:half, :], preferred_element_type=F32)
    y = y + jnp.dot(r_ref[...], w_ref[half:, :], preferred_element_type=F32)
    x = x_ref[...] + y
    xo_ref[...] = x
    xn_ref[...] = _rms(x, g_ref[...], EPS).astype(xn_ref.dtype)


def _out_proj(attn, lru, w_all, x, g_all, layer, bm):
    m, half = attn.shape
    d = x.shape[1]
    return pl.pallas_call(
        _out_proj_kernel,
        grid=(m // bm,),
        in_specs=[pl.BlockSpec((bm, half), lambda i: (i, 0)),
                  pl.BlockSpec((bm, half), lambda i: (i, 0)),
                  pl.BlockSpec((None, 2 * half, d), lambda i: (layer, 0, 0)),
                  pl.BlockSpec((bm, d), lambda i: (i, 0)),
                  pl.BlockSpec((None, 1, d), lambda i: (layer, 0, 0))],
        out_specs=[pl.BlockSpec((bm, d), lambda i: (i, 0)),
                   pl.BlockSpec((bm, d), lambda i: (i, 0))],
        out_shape=[jax.ShapeDtypeStruct((m, d), F32), jax.ShapeDtypeStruct((m, d), BF16)],
        compiler_params=_params("arbitrary"),
        name="out_proj",
    )(attn, lru, w_all, x, g_all)


def _mlp_up_kernel(xn_ref, w_ref, h_ref):
    y = jnp.dot(xn_ref[...], w_ref[...], preferred_element_type=F32)
    h_ref[...] = jnp.square(jnp.maximum(y, 0.0)).astype(h_ref.dtype)


def _mlp_up(xn, w_all, layer, bm, bn):
    m, d = xn.shape
    f = w_all.shape[2]
    return pl.pallas_call(
        _mlp_up_kernel,
        grid=(m // bm, f // bn),
        in_specs=[pl.BlockSpec((bm, d), lambda i, j: (i, 0)),
                  pl.BlockSpec((None, d, bn), lambda i, j: (layer, 0, j))],
        out_specs=pl.BlockSpec((bm, bn), lambda i, j: (i, j)),
        out_shape=jax.ShapeDtypeStruct((m, f), BF16),
        compiler_params=_params("arbitrary", "arbitrary"),
        name="mlp_up",
    )(xn, w_all)


def _mlp_down_kernel(h_ref, w_ref, x_ref, g_ref, xo_ref, xn_ref):
    kk = pl.program_id(1)
    y = jnp.dot(h_ref[...], w_ref[...], preferred_element_type=F32)

    @pl.when(kk == 0)
    def _():
        xo_ref[...] = x_ref[...] + y

    @pl.when(kk > 0)
    def _():
        xo_ref[...] += y

    @pl.when(kk == pl.num_programs(1) - 1)
    def _():
        xn_ref[...] = _rms(xo_ref[...], g_ref[...], EPS).astype(xn_ref.dtype)


def _mlp_down(h, w_all, layer, x, g, norm_dtype, bm, bk):
    m, f = h.shape
    d = x.shape[1]
    return pl.pallas_call(
        _mlp_down_kernel,
        grid=(m // bm, f // bk),
        in_specs=[pl.BlockSpec((bm, bk), lambda i, k: (i, k)),
                  pl.BlockSpec((None, bk, d), lambda i, k: (layer, k, 0)),
                  pl.BlockSpec((bm, d), lambda i, k: (i, 0)),
                  pl.BlockSpec((1, d), lambda i, k: (0, 0))],
        out_specs=[pl.BlockSpec((bm, d), lambda i, k: (i, 0)),
                   pl.BlockSpec((bm, d), lambda i, k: (i, 0))],
        out_shape=[jax.ShapeDtypeStruct((m, d), F32), jax.ShapeDtypeStruct((m, d), norm_dtype)],
        compiler_params=_params("arbitrary", "arbitrary"),
        name="mlp_down",
    )(h, w_all, x, g)


def _rope_tables(positions, dqk):
    rot = dqk // ROT_FRACTION
    inv_freq = ROPE_THETA ** (-jnp.arange(0, rot, 2, dtype=F32) / rot)
    ang = positions.astype(F32)[:, None] * inv_freq[None, :]
    cos, sin = jnp.cos(ang), jnp.sin(ang)
    n = positions.shape[0]
    zh = jnp.zeros((n, rot // 2), F32)
    rest0 = jnp.zeros((n, dqk - rot), F32)
    c = jnp.concatenate([cos, cos, jnp.ones((n, dqk - rot), F32)], axis=1)
    s1 = jnp.concatenate([-sin, zh, rest0], axis=1)
    s2 = jnp.concatenate([zh, sin, rest0], axis=1)
    reps = LANES_V7X // dqk
    return tuple(jnp.tile(t, (1, reps)) for t in (c, s1, s2))


def _trunk(x, tables, conv0, h0, P, attend, lru, bm, bm_down, bk_down):
    depth = P["w_in"].shape[0]
    dqk = P["lam"].shape[2]
    q_scale = dqk ** -0.5
    xn = _rmsnorm(x, P["norm_mix_g"], 0, bm)
    ks, vs, hts, bufs = [], [], [], []
    y = None
    for l in range(depth):
        lambda_init = 0.8 - 0.6 * math.exp(-0.3 * l)
        q, k, v, xr, gr = _in_proj(xn, P["w_in"], l, tables, bm, q_scale)
        attn = attend(l, q, k, v, lambda_init)
        lru_out, ht, buf = lru(l, xr, gr, conv0[l], h0[l])
        x, xn = _out_proj(attn, lru_out, P["w_out"], x, P["norm_mlp_g"], l, bm_down)
        hmid = _mlp_up(xn, P["w_up"], l, bm, 1024)
        last = l == depth - 1
        g_next = P["final_norm_g"] if last else P["norm_mix_g"][l + 1]
        x, xn = _mlp_down(hmid, P["w_down"], l, x, g_next, F32 if last else BF16, bm_down, bk_down)
        y = xn
        ks.append(k)
        vs.append(v)
        hts.append(ht)
        bufs.append(buf)
    return y, jnp.stack(ks), jnp.stack(vs), jnp.stack(hts), jnp.stack(bufs)


def kernel(x_prompt, x_sample, cache_k, cache_v, state_h, state_conv, page_table, norm_mix_g, w_in, lambda_q1, lambda_k1, lambda_q2, lambda_k2, subln_g, conv_w, conv_b, w_gate_a, b_gate_a, w_gate_x, b_gate_x, lru_lambda, w_out, norm_mlp_g, w_up, w_down, final_norm_g):
    batch, seq, d_model = x_prompt.shape
    nb, t_new, _ = x_sample.shape
    depth, _, page, n_maps, dqk = cache_k.shape
    dv = cache_v.shape[-1]
    d_lru = state_h.shape[-1]
    taps = conv_w.shape[1]
    assert t_new == 1, "the sample kernels handle one new token per sequence"
    past_len = page_table.shape[1] * page

    vec = lambda a: a.reshape(depth, 1, a.shape[-1])
    P = {
        "w_in": w_in.astype(BF16), "w_out": w_out.astype(BF16),
        "w_up": w_up.astype(BF16), "w_down": w_down.astype(BF16),
        "norm_mix_g": vec(norm_mix_g), "norm_mlp_g": vec(norm_mlp_g),
        "final_norm_g": final_norm_g.reshape(1, d_model),
        "lam": jnp.stack([lambda_q1, lambda_k1, lambda_q2, lambda_k2], axis=1),
        "subln_g": vec(subln_g),
    }
    lw = (conv_w, vec(conv_b), jnp.concatenate([w_gate_a, w_gate_x], axis=-1).astype(BF16),
          vec(b_gate_a), vec(b_gate_x), vec(lru_lambda))

    bm = 1024
    tables = _rope_tables(jnp.arange(seq, dtype=jnp.int32), dqk)

    def attend_prompt(l, q, k, v, lambda_init):
        return _prompt_attention(q, k, v, P["lam"], P["subln_g"], l, batch, seq, 512, lambda_init)

    def lru_prompt(l, xr, gr, conv0, h0):
        return _prompt_lru(xr, gr, conv0, h0, lw, l, batch, seq, 512)

    zero_conv = jnp.zeros((depth, batch, taps - 1, d_lru), F32)
    zero_h = jnp.zeros((depth, batch, d_lru), F32)
    y_p, k_p, v_p, h_p, c_p = _trunk(x_prompt.reshape(batch * seq, d_model), tables, zero_conv, zero_h,
                                     P, attend_prompt, lru_prompt, bm, 512, 2048)

    pos_s = jnp.broadcast_to(past_len + jnp.arange(t_new, dtype=jnp.int32), (nb,))
    tables_s = _rope_tables(pos_s, dqk)

    def attend_sample(l, q, k, v, lambda_init):
        return _decode_attention(q, k, v, cache_k, cache_v, page_table, P["lam"], P["subln_g"], l,
                                 lambda_init)

    def lru_sample(l, xr, gr, conv_t, h0):
        out, ht, buf = _sample_lru(xr, gr, conv_t, h0, lw, l)
        return out, ht, buf.swapaxes(0, 1)

    conv_t = state_conv.swapaxes(1, 2)
    y_s, k_s, v_s, h_s, c_s = _trunk(x_sample.reshape(nb, d_model), tables_s, conv_t, state_h,
                                     P, attend_sample, lru_sample, nb, nb, 2048)

    return (y_p.reshape(batch, seq, d_model),
            y_s.reshape(nb, t_new, d_model),
            k_p.reshape(depth, batch, seq, n_maps, dqk),
            v_p.reshape(depth, batch, seq, N_HEADS, dv),
            h_p.reshape(depth, batch, d_lru),
            c_p,
            k_s.reshape(depth, nb, t_new, n_maps, dqk),
            v_s.reshape(depth, nb, t_new, N_HEADS, dv),
            h_s,
            c_s)
```

```python
import functools
import math

import jax
import jax.numpy as jnp
from jax import lax
from jax.experimental import pallas as pl
from jax.experimental.pallas import tpu as pltpu

F32 = jnp.float32
BF16 = jnp.bfloat16

LANES_V7X = 128
VMEM_LIMIT_BYTES_V7X = 56 * 1024 * 1024

N_HEADS = 8
ROPE_THETA = 500000.0
ROT_FRACTION = 4
LRU_C = 8.0
EPS = 1e-6
SUBLN_EPS = 1e-5
NEG_INF = -1e30
PAGES_PER_STEP = 8


def _params(*sem):
    return pltpu.CompilerParams(dimension_semantics=sem, vmem_limit_bytes=VMEM_LIMIT_BYTES_V7X)


def _rms(y, g, eps):
    return y * lax.rsqrt(jnp.mean(y * y, axis=-1, keepdims=True) + eps) * g


def _rmsnorm_kernel(x_ref, g_ref, o_ref):
    o_ref[...] = _rms(x_ref[...], g_ref[...], EPS).astype(o_ref.dtype)


def _rmsnorm(x, g_all, layer, bm):
    m, d = x.shape
    return pl.pallas_call(
        _rmsnorm_kernel,
        grid=(m // bm,),
        in_specs=[pl.BlockSpec((bm, d), lambda i: (i, 0)),
                  pl.BlockSpec((None, 1, d), lambda i: (layer, 0, 0))],
        out_specs=pl.BlockSpec((bm, d), lambda i: (i, 0)),
        out_shape=jax.ShapeDtypeStruct((m, d), BF16),
        compiler_params=_params("arbitrary"),
        name="rmsnorm",
    )(x, g_all)


def _in_proj_qk_kernel(xn_ref, w_ref, c_ref, s1_ref, s2_ref, q_ref, k_ref, *, q_scale):
    z = jnp.dot(xn_ref[...], w_ref[...], preferred_element_type=F32)
    width = q_ref.shape[1]
    c, s1, s2 = c_ref[...], s1_ref[...], s2_ref[...]

    def rope_store(dst_ref, base, mul):
        for t in range(width // LANES_V7X):
            zt = z[:, base + t * LANES_V7X:base + (t + 1) * LANES_V7X]
            r = (zt * c + pltpu.roll(zt, LANES_V7X - 8, 1) * s1) + pltpu.roll(zt, 8, 1) * s2
            dst_ref[:, t * LANES_V7X:(t + 1) * LANES_V7X] = (r * mul).astype(dst_ref.dtype)

    rope_store(q_ref, 0, q_scale)
    rope_store(k_ref, width, 1.0)


def _in_proj_rest_kernel(xn_ref, w_ref, *out_refs):
    j = pl.program_id(1)
    z = jnp.dot(xn_ref[...], w_ref[...], preferred_element_type=F32)
    for idx, o_ref in enumerate(out_refs):
        @pl.when(j == idx)
        def _(o_ref=o_ref):
            o_ref[...] = z


def _in_proj(xn, w_all, layer, tables, bm, q_scale):
    m, d = xn.shape
    n_sec = 5
    width = w_all.shape[2] // n_sec
    c, s1, s2 = tables
    n_tbl = c.shape[0] // bm
    tbl_spec = pl.BlockSpec((bm, LANES_V7X), lambda i, j: (i % n_tbl, 0))
    x_spec = pl.BlockSpec((bm, d), lambda i, j: (i, 0))
    out_spec = pl.BlockSpec((bm, width), lambda i, j: (i, 0))
    f32_out = jax.ShapeDtypeStruct((m, width), F32)
    q, k = pl.pallas_call(
        functools.partial(_in_proj_qk_kernel, q_scale=q_scale),
        grid=(m // bm, 1),
        in_specs=[x_spec, pl.BlockSpec((None, d, 2 * width), lambda i, j: (layer, 0, 0)),
                  tbl_spec, tbl_spec, tbl_spec],
        out_specs=[out_spec] * 2,
        out_shape=[jax.ShapeDtypeStruct((m, width), BF16), f32_out],
        compiler_params=_params("arbitrary", "arbitrary"),
        name="in_proj_qk",
    )(xn, w_all, c, s1, s2)
    v, xr, gr = pl.pallas_call(
        _in_proj_rest_kernel,
        grid=(m // bm, n_sec - 2),
        in_specs=[x_spec, pl.BlockSpec((None, d, width), lambda i, j: (layer, 0, j + 2))],
        out_specs=[out_spec] * (n_sec - 2),
        out_shape=[f32_out] * (n_sec - 2),
        compiler_params=_params("arbitrary", "arbitrary"),
        name="in_proj_rest",
    )(xn, w_all)
    return q, k, v, xr, gr


def _diff_lambda(lam_ref, lambda_init):
    lv = lam_ref[...]
    l1 = jnp.sum(lv[0:1] * lv[1:2], axis=1, keepdims=True)
    l2 = jnp.sum(lv[2:3] * lv[3:4], axis=1, keepdims=True)
    return jnp.exp(l1) - jnp.exp(l2) + lambda_init


def _diff_combine(o0, o1, lam, g, lambda_init):
    d = o0 - lam * o1
    d = d * lax.rsqrt(jnp.mean(d * d, axis=-1, keepdims=True) + SUBLN_EPS)
    return d * g * (1.0 - lambda_init)


def _reduce_rows(x, op, reduce):
    rows = x.shape[0]
    while rows > 8 and rows % 16 == 0:
        rows //= 2
        x = op(x[:rows], x[rows:])
    return reduce(x, axis=0, keepdims=True)


def _attn_kernel(q_ref, k_ref, v_ref, lam_ref, g_ref, o_ref,
                 kb_ref, vt_ref, qt_ref, s0_ref, s1_ref, m_ref, l_ref, acc_ref,
                 *, bq, cw, dqk, lambda_init):
    qi = pl.program_id(2)
    nk = kb_ref.shape[0]

    @pl.when(qi == 0)
    def _():
        for j in range(nk):
            kb_ref[j] = k_ref[j * bq:(j + 1) * bq, :].astype(BF16)
            vt_ref[j] = v_ref[j * bq:(j + 1) * bq, :].T.astype(BF16)

    qt = q_ref[...].astype(F32).T
    row = lax.broadcasted_iota(jnp.int32, qt.shape, 0)
    qt_ref[:, :bq] = jnp.where(row < dqk, qt, 0.0).astype(BF16)
    qt_ref[:, bq:] = jnp.where(row >= dqk, qt, 0.0).astype(BF16)

    m_ref[...] = jnp.full(m_ref.shape, NEG_INF, F32)
    l_ref[...] = jnp.zeros(l_ref.shape, F32)
    acc_ref[...] = jnp.zeros(acc_ref.shape, F32)

    def scores(kj, s_ref):
        s_ref[...] = jnp.dot(kb_ref[kj], qt_ref[...], preferred_element_type=F32)

    def softmax_pv(kj, s_ref, masked):
        for c in range(2 * bq // cw):
            cs = slice(c * cw, (c + 1) * cw)
            q_lo = (c * cw) % bq
            klen = min(bq, q_lo + cw) if masked else bq
            s = s_ref[:klen, cs]
            if masked:
                key = lax.broadcasted_iota(jnp.int32, s.shape, 0)
                qry = (lax.broadcasted_iota(jnp.int32, s.shape, 1) + c * cw) & (bq - 1)
                s = jnp.where(key <= qry, s, NEG_INF)
            m_prev = m_ref[:, cs]
            m_new = jnp.maximum(m_prev, _reduce_rows(s, jnp.maximum, jnp.max))
            alpha = jnp.exp(m_prev - m_new)
            p = jnp.exp(s - m_new)
            l_ref[:, cs] = alpha * l_ref[:, cs] + _reduce_rows(p, jnp.add, jnp.sum)
            pv = jnp.dot(vt_ref[kj, :, :klen], p.astype(BF16), preferred_element_type=F32)
            acc_ref[:, cs] = alpha * acc_ref[:, cs] + pv
            m_ref[:, cs] = m_new

    def stage(kj, masked):
        for parity, (cur, nxt) in enumerate(((s0_ref, s1_ref), (s1_ref, s0_ref))):
            @pl.when((kj & 1) == parity)
            def _(cur=cur, nxt=nxt):
                if not masked:
                    scores(kj + 1, nxt)
                softmax_pv(kj, cur, masked)

    scores(0, s0_ref)

    def body(kj, carry):
        stage(kj, False)
        return carry

    lax.fori_loop(0, qi, body, 0)
    stage(qi, True)

    o = (acc_ref[...] / l_ref[...]).T
    lam = _diff_lambda(lam_ref, lambda_init)
    o_ref[...] = _diff_combine(o[:bq], o[bq:], lam, g_ref[...], lambda_init).astype(o_ref.dtype)


def _prompt_attention(q, k, v, lam_all, g_all, layer, batch, seq, bq, lambda_init):
    m, width = q.shape
    dv = width // N_HEADS
    dqk = dv // 2
    nq = seq // bq
    cw = min(bq, 4 * LANES_V7X)
    assert seq % bq == 0 and bq % cw == 0 and (bq & (bq - 1)) == 0
    return pl.pallas_call(
        functools.partial(_attn_kernel, bq=bq, cw=cw, dqk=dqk, lambda_init=lambda_init),
        grid=(batch, N_HEADS, nq),
        in_specs=[pl.BlockSpec((bq, dv), lambda b, h, i: (b * nq + i, h)),
                  pl.BlockSpec((seq, dv), lambda b, h, i: (b, h)),
                  pl.BlockSpec((seq, dv), lambda b, h, i: (b, h)),
                  pl.BlockSpec((None, 4, dqk), lambda b, h, i: (layer, 0, 0)),
                  pl.BlockSpec((None, 1, dv), lambda b, h, i: (layer, 0, 0))],
        out_specs=pl.BlockSpec((bq, dv), lambda b, h, i: (b * nq + i, h)),
        out_shape=jax.ShapeDtypeStruct((m, width), BF16),
        scratch_shapes=[pltpu.VMEM((nq, bq, dv), BF16),
                        pltpu.VMEM((nq, dv, bq), BF16),
                        pltpu.VMEM((dv, 2 * bq), BF16),
                        pltpu.VMEM((bq, 2 * bq), F32),
                        pltpu.VMEM((bq, 2 * bq), F32),
                        pltpu.VMEM((1, 2 * bq), F32),
                        pltpu.VMEM((1, 2 * bq), F32),
                        pltpu.VMEM((dv, 2 * bq), F32)],
        compiler_params=_params("arbitrary", "arbitrary", "arbitrary"),
        name="prompt_attention",
    )(q, k, v, lam_all, g_all)


def _mlp_up_decode_kernel(pt_ref, xn_ref, w_ref, q_ref, kn_ref, vn_ref, lam_ref, g_ref, *rest,
                          lambda_init):
    del pt_ref
    k_refs = rest[:PAGES_PER_STEP]
    v_refs = rest[PAGES_PER_STEP:2 * PAGES_PER_STEP]
    h_ref, o_ref, qm_ref, m_ref, l_ref, acc_ref = rest[2 * PAGES_PER_STEP:]
    p_idx = pl.program_id(1)
    n_maps, dqk, page = k_refs[0].shape
    n_heads = n_maps // 2
    width = n_maps * dqk

    @pl.when(p_idx == 0)
    def _():
        row = lax.broadcasted_iota(jnp.int32, (n_maps, width), 0)
        lane = lax.broadcasted_iota(jnp.int32, (n_maps, width), 1)
        own_map = (lane >= row * dqk) & (lane < (row + 1) * dqk)
        qrow = jnp.broadcast_to(q_ref[0].astype(F32), (n_maps, width))
        qm = jnp.where(own_map, qrow, 0.0)
        qm_ref[...] = qm.astype(BF16)
        kn = jnp.broadcast_to(kn_ref[0].astype(BF16).astype(F32), (n_maps, width))
        m_ref[...] = jnp.sum(qm * kn, axis=1, keepdims=True)
        l_ref[...] = jnp.ones(l_ref.shape, F32)
        vn = vn_ref[0].astype(BF16).astype(F32)
        acc_ref[pl.ds(0, n_heads, stride=2), :] = vn
        acc_ref[pl.ds(1, n_heads, stride=2), :] = vn

    qm = qm_ref[...]
    scores = []
    for kp_ref in k_refs:
        kt = kp_ref[...].reshape(width, -1).astype(BF16)
        scores.append(jnp.dot(qm, kt, preferred_element_type=F32))
    _mlp_up_kernel(xn_ref, w_ref, h_ref)
    m_prev = m_ref[...]
    m_new = m_prev
    for s in scores:
        m_new = jnp.maximum(m_new, jnp.max(s, axis=1, keepdims=True))
    alpha = jnp.exp(m_prev - m_new)
    l_new = alpha * l_ref[...]
    acc = alpha * acc_ref[...]
    own_head = lax.shift_right_logical(lax.broadcasted_iota(jnp.int32, acc.shape, 0), 1)
    for s, vp_ref in zip(scores, v_refs):
        p = jnp.exp(s - m_new)
        l_new = l_new + jnp.sum(p, axis=1, keepdims=True)
        pb = p.astype(BF16)
        for h in range(n_heads):
            vh = vp_ref[pl.ds(h, page, stride=n_heads), :].astype(BF16)
            pv = jnp.dot(pb, vh, preferred_element_type=F32)
            acc = acc + jnp.where(own_head == h, pv, 0.0)
    l_ref[...] = l_new
    acc_ref[...] = acc
    m_ref[...] = m_new

    @pl.when(p_idx == pl.num_programs(1) - 1)
    def _():
        acc_ref[...] = acc_ref[...] / l_ref[...]
        o0 = acc_ref[pl.ds(0, n_heads, stride=2), :]
        o1 = acc_ref[pl.ds(1, n_heads, stride=2), :]
        lam = _diff_lambda(lam_ref, lambda_init)
        o_ref[0] = _diff_combine(o0, o1, lam, g_ref[...], lambda_init).astype(o_ref.dtype)


def _mlp_up_decode(xn, w_all, q, k_new, v_new, cache_k, cache_v, page_table, lam_all, g_all, layer,
                   lambda_init):
    nb, width = q.shape
    depth, n_pool, page, n_maps, dqk = cache_k.shape
    n_heads, dv = cache_v.shape[-2:]
    n_pages = page_table.shape[1]
    m, d = xn.shape
    f = w_all.shape[2]
    n_steps = n_pages // PAGES_PER_STEP
    bm, bn = m // nb, f // n_steps
    assert n_pages % PAGES_PER_STEP == 0 and n_maps == 2 * n_heads and (n_heads & (n_heads - 1)) == 0
    assert m % nb == 0 and f % n_steps == 0 and bm % 16 == 0 and bn % LANES_V7X == 0
    ckt = cache_k.transpose(0, 1, 3, 4, 2)
    cv = cache_v.reshape(depth, n_pool, page * n_heads, dv)

    def page_spec(*block):
        zeros = (0,) * len(block)
        return [pl.BlockSpec((None, None) + block,
                             functools.partial(lambda b, p, pt, i: (layer, pt[b, p * PAGES_PER_STEP + i]) + zeros, i=i))
                for i in range(PAGES_PER_STEP)]

    row_spec = pl.BlockSpec((1, 1, width), lambda b, p, pt: (b, 0, 0))
    head_spec = pl.BlockSpec((1, n_heads, dv), lambda b, p, pt: (b, 0, 0))
    grid_spec = pltpu.PrefetchScalarGridSpec(
        num_scalar_prefetch=1,
        grid=(nb, n_steps),
        in_specs=[pl.BlockSpec((bm, d), lambda b, p, pt: (b, 0)),
                  pl.BlockSpec((None, d, bn), lambda b, p, pt: (layer, 0, p)),
                  row_spec, row_spec, head_spec,
                  pl.BlockSpec((None, 4, dqk), lambda b, p, pt: (layer, 0, 0)),
                  pl.BlockSpec((None, 1, dv), lambda b, p, pt: (layer, 0, 0))]
                 + page_spec(n_maps, dqk, page) + page_spec(page * n_heads, dv),
        out_specs=[pl.BlockSpec((bm, bn), lambda b, p, pt: (b, p)), head_spec],
        scratch_shapes=[pltpu.VMEM((n_maps, width), BF16),
                        pltpu.VMEM((n_maps, 1), F32),
                        pltpu.VMEM((n_maps, 1), F32),
                        pltpu.VMEM((n_maps, dv), F32)],
    )
    hmid, out = pl.pallas_call(
        functools.partial(_mlp_up_decode_kernel, lambda_init=lambda_init),
        grid_spec=grid_spec,
        out_shape=[jax.ShapeDtypeStruct((m, f), BF16),
                   jax.ShapeDtypeStruct((nb, n_heads, dv), BF16)],
        compiler_params=_params("arbitrary", "arbitrary"),
        name="mlp_up_decode",
    )(page_table, xn, w_all, q.reshape(nb, 1, width), k_new.reshape(nb, 1, width),
      v_new.reshape(nb, n_heads, dv), lam_all, g_all,
      *([ckt] * PAGES_PER_STEP), *([cv] * PAGES_PER_STEP))
    return hmid, out.reshape(nb, width)


def _lru_gates(xc, wg_ref, ba_ref, bx_ref, sp, store):
    xcb = xc.astype(BF16)
    bw = wg_ref.shape[1]
    for hb in range(wg_ref.shape[0]):
        sl = slice(hb * bw, (hb + 1) * bw)
        y = jnp.dot(xcb[:, sl], wg_ref[hb], preferred_element_type=F32)
        r = jax.nn.sigmoid(y[:, :bw] + ba_ref[:, sl])
        i = jax.nn.sigmoid(y[:, bw:] + bx_ref[:, sl])
        log_a = -LRU_C * r * sp[:, sl]
        a = jnp.exp(log_a)
        u = jnp.sqrt(-jnp.tanh(log_a) * (a * a + 1.0)) * (i * xc[:, sl])
        store(sl, a, u)


def _lru_kernel(xr_ref, gr_ref, conv0_ref, h0_ref, cw_ref, cb_ref, wg_ref, ba_ref, bx_ref, lam_ref,
                out_ref, ht_ref, cv_ref, xp_ref, a_ref, u_ref, h_ref, *, ts):
    ti = pl.program_id(1)
    taps = cw_ref.shape[0]
    pad = 8
    hist = pad - (taps - 1)

    @pl.when(ti == 0)
    def _():
        xp_ref[hist:pad, :] = conv0_ref[0]
        h_ref[...] = h0_ref[0]

    xp_ref[pad:pad + ts, :] = xr_ref[...]
    cw = cw_ref[...]
    conv = cw[0:1] * xp_ref[hist:hist + ts, :]
    for j in range(1, taps):
        conv = conv + cw[j:j + 1] * xp_ref[hist + j:hist + j + ts, :]
    xc = cb_ref[...] + conv
    sp = jax.nn.softplus(-lam_ref[...])

    def store(sl, a, u):
        a_ref[:, sl] = a
        u_ref[:, sl] = u

    _lru_gates(xc, wg_ref, ba_ref, bx_ref, sp, store)

    def step(t, h):
        h = a_ref[pl.ds(t, 1), :] * h + u_ref[pl.ds(t, 1), :]
        u_ref[pl.ds(t, 1), :] = h
        return h

    h_last = lax.fori_loop(0, ts, step, h_ref[...], unroll=8)
    h_ref[...] = h_last
    ht_ref[0] = h_last
    tail = xp_ref[hist + ts:pad + ts, :]
    cv_ref[0] = tail
    xp_ref[hist:pad, :] = tail
    out_ref[...] = (u_ref[...] * jax.nn.gelu(gr_ref[...])).astype(out_ref.dtype)


def _lru_weight_specs(layer, taps, d, nblk, bw, nargs):
    idx = {1: lambda b: (layer, 0, 0), 2: lambda b, t: (layer, 0, 0)}[nargs]
    idx4 = {1: lambda b: (layer, 0, 0, 0), 2: lambda b, t: (layer, 0, 0, 0)}[nargs]
    vec = pl.BlockSpec((None, 1, d), idx)
    return [pl.BlockSpec((None, taps, d), idx), vec,
            pl.BlockSpec((None, nblk, bw, 2 * bw), idx4), vec, vec, vec]


def _prompt_lru(xr, gr, conv0, h0, lw, layer, batch, seq, ts):
    m, d = xr.shape
    nt = seq // ts
    taps = lw[0].shape[1]
    nblk, bw = lw[2].shape[1], lw[2].shape[2]
    tile = pl.BlockSpec((ts, d), lambda b, t: (b * nt + t, 0))
    return pl.pallas_call(
        functools.partial(_lru_kernel, ts=ts),
        grid=(batch, nt),
        in_specs=[tile, tile,
                  pl.BlockSpec((1, taps - 1, d), lambda b, t: (b, 0, 0)),
                  pl.BlockSpec((1, 1, d), lambda b, t: (b, 0, 0))]
                 + _lru_weight_specs(layer, taps, d, nblk, bw, 2),
        out_specs=[tile,
                   pl.BlockSpec((1, 1, d), lambda b, t: (b, 0, 0)),
                   pl.BlockSpec((1, taps - 1, d), lambda b, t: (b, 0, 0))],
        out_shape=[jax.ShapeDtypeStruct((m, d), BF16),
                   jax.ShapeDtypeStruct((batch, 1, d), F32),
                   jax.ShapeDtypeStruct((batch, taps - 1, d), F32)],
        scratch_shapes=[pltpu.VMEM((ts + 8, d), F32),
                        pltpu.VMEM((ts, d), F32),
                        pltpu.VMEM((ts, d), F32),
                        pltpu.VMEM((1, d), F32)],
        compiler_params=_params("arbitrary", "arbitrary"),
        name="prompt_lru",
    )(xr, gr, conv0, h0.reshape(batch, 1, d), *lw)


def _sample_lru_kernel(xr_ref, gr_ref, conv_ref, h0_ref, cw_ref, cb_ref, wg_ref, ba_ref, bx_ref,
                       lam_ref, out_ref, ht_ref, cv_ref):
    taps = cw_ref.shape[0]
    xr = xr_ref[...]
    cw = cw_ref[...]
    conv = cw[0:1] * conv_ref[0]
    for j in range(1, taps - 1):
        conv = conv + cw[j:j + 1] * conv_ref[j]
    conv = conv + cw[taps - 1:taps] * xr
    xc = cb_ref[...] + conv
    sp = jax.nn.softplus(-lam_ref[...])
    h0 = h0_ref[...]
    gate = jax.nn.gelu(gr_ref[...])

    def store(sl, a, u):
        h = a * h0[:, sl] + u
        ht_ref[:, sl] = h
        out_ref[:, sl] = (h * gate[:, sl]).astype(out_ref.dtype)

    _lru_gates(xc, wg_ref, ba_ref, bx_ref, sp, store)
    for j in range(taps - 2):
        cv_ref[j] = conv_ref[j + 1]
    cv_ref[taps - 2] = xr


def _sample_lru(xr, gr, conv_t, h0, lw, layer):
    nb, d = xr.shape
    taps = lw[0].shape[1]
    nblk, bw = lw[2].shape[1], lw[2].shape[2]
    row = pl.BlockSpec((nb, d), lambda b: (0, 0))
    hist = pl.BlockSpec((taps - 1, nb, d), lambda b: (0, 0, 0))
    return pl.pallas_call(
        _sample_lru_kernel,
        grid=(1,),
        in_specs=[row, row, hist, row] + _lru_weight_specs(layer, taps, d, nblk, bw, 1),
        out_specs=[row, row, hist],
        out_shape=[jax.ShapeDtypeStruct((nb, d), BF16),
                   jax.ShapeDtypeStruct((nb, d), F32),
                   jax.ShapeDtypeStruct((taps - 1, nb, d), F32)],
        compiler_params=_params("arbitrary"),
        name="sample_lru",
    )(xr, gr, conv_t, h0, *lw)


def _out_proj_kernel(a_ref, r_ref, w_ref, x_ref, g_ref, xo_ref, xn_ref):
    half = a_ref.shape[1]
    y = jnp.dot(a_ref[...], w_ref[:half, :], preferred_element_type=F32)
    y = y + jnp.dot(r_ref[...], w_ref[half:, :], preferred_element_type=F32)
    x = x_ref[...] + y
    xo_ref[...] = x
    xn_ref[...] = _rms(x, g_ref[...], EPS).astype(xn_ref.dtype)


def _out_proj(attn, lru, w_all, x, g_all, layer, bm):
    m, half = attn.shape
    d = x.shape[1]
    return pl.pallas_call(
        _out_proj_kernel,
        grid=(m // bm,),
        in_specs=[pl.BlockSpec((bm, half), lambda i: (i, 0)),
                  pl.BlockSpec((bm, half), lambda i: (i, 0)),
                  pl.BlockSpec((None, 2 * half, d), lambda i: (layer, 0, 0)),
                  pl.BlockSpec((bm, d), lambda i: (i, 0)),
                  pl.BlockSpec((None, 1, d), lambda i: (layer, 0, 0))],
        out_specs=[pl.BlockSpec((bm, d), lambda i: (i, 0)),
                   pl.BlockSpec((bm, d), lambda i: (i, 0))],
        out_shape=[jax.ShapeDtypeStruct((m, d), F32), jax.ShapeDtypeStruct((m, d), BF16)],
        compiler_params=_params("arbitrary"),
        name="out_proj",
    )(attn, lru, w_all, x, g_all)


def _mlp_up_kernel(xn_ref, w_ref, h_ref):
    y = jnp.dot(xn_ref[...], w_ref[...], preferred_element_type=F32)
    h_ref[...] = jnp.square(jnp.maximum(y, 0.0)).astype(h_ref.dtype)


def _mlp_up(xn, w_all, layer, bm, bn):
    m, d = xn.shape
    f = w_all.shape[2]
    return pl.pallas_call(
        _mlp_up_kernel,
        grid=(m // bm, f // bn),
        in_specs=[pl.BlockSpec((bm, d), lambda i, j: (i, 0)),
                  pl.BlockSpec((None, d, bn), lambda i, j: (layer, 0, j))],
        out_specs=pl.BlockSpec((bm, bn), lambda i, j: (i, j)),
        out_shape=jax.ShapeDtypeStruct((m, f), BF16),
        compiler_params=_params("arbitrary", "arbitrary"),
        name="mlp_up",
    )(xn, w_all)


def _mlp_down_kernel(h_ref, w_ref, x_ref, g_ref, xo_ref, xn_ref):
    kk = pl.program_id(1)

    @pl.when(kk == 0)
    def _():
        xo_ref[...] = x_ref[...]

    xo_ref[...] += jnp.dot(h_ref[...], w_ref[...], preferred_element_type=F32)

    @pl.when(kk == pl.num_programs(1) - 1)
    def _():
        xn_ref[...] = _rms(xo_ref[...], g_ref[...], EPS).astype(xn_ref.dtype)


def _mlp_down(h, w_all, layer, x, g, norm_dtype, bm, bk):
    m, f = h.shape
    d = x.shape[1]
    return pl.pallas_call(
        _mlp_down_kernel,
        grid=(m // bm, f // bk),
        in_specs=[pl.BlockSpec((bm, bk), lambda i, k: (i, k)),
                  pl.BlockSpec((None, bk, d), lambda i, k: (layer, k, 0)),
                  pl.BlockSpec((bm, d), lambda i, k: (i, 0)),
                  pl.BlockSpec((1, d), lambda i, k: (0, 0))],
        out_specs=[pl.BlockSpec((bm, d), lambda i, k: (i, 0)),
                   pl.BlockSpec((bm, d), lambda i, k: (i, 0))],
        out_shape=[jax.ShapeDtypeStruct((m, d), F32), jax.ShapeDtypeStruct((m, d), norm_dtype)],
        compiler_params=_params("arbitrary", "arbitrary"),
        name="mlp_down",
    )(h, w_all, x, g)


def _rope_tables(positions, dqk):
    rot = dqk // ROT_FRACTION
    inv_freq = ROPE_THETA ** (-jnp.arange(0, rot, 2, dtype=F32) / rot)
    ang = positions.astype(F32)[:, None] * inv_freq[None, :]
    cos, sin = jnp.cos(ang), jnp.sin(ang)
    n = positions.shape[0]
    zh = jnp.zeros((n, rot // 2), F32)
    rest0 = jnp.zeros((n, dqk - rot), F32)
    c = jnp.concatenate([cos, cos, jnp.ones((n, dqk - rot), F32)], axis=1)
    s1 = jnp.concatenate([-sin, zh, rest0], axis=1)
    s2 = jnp.concatenate([zh, sin, rest0], axis=1)
    reps = LANES_V7X // dqk
    return tuple(jnp.tile(t, (1, reps)) for t in (c, s1, s2))


def kernel(x_prompt, x_sample, cache_k, cache_v, state_h, state_conv, page_table, norm_mix_g, w_in, lambda_q1, lambda_k1, lambda_q2, lambda_k2, subln_g, conv_w, conv_b, w_gate_a, b_gate_a, w_gate_x, b_gate_x, lru_lambda, w_out, norm_mlp_g, w_up, w_down, final_norm_g):
    batch, seq, d_model = x_prompt.shape
    nb, t_new, _ = x_sample.shape
    depth, _, page, n_maps, dqk = cache_k.shape
    dv = cache_v.shape[-1]
    d_lru = state_h.shape[-1]
    taps = conv_w.shape[1]
    assert t_new == 1, "the sample kernels handle one new token per sequence"
    past_len = page_table.shape[1] * page

    vec = lambda a: a.reshape(depth, 1, a.shape[-1])
    P = {
        "w_in": w_in.astype(BF16), "w_out": w_out.astype(BF16),
        "w_up": w_up.astype(BF16), "w_down": w_down.astype(BF16),
        "norm_mix_g": vec(norm_mix_g), "norm_mlp_g": vec(norm_mlp_g),
        "final_norm_g": final_norm_g.reshape(1, d_model),
        "lam": jnp.stack([lambda_q1, lambda_k1, lambda_q2, lambda_k2], axis=1),
        "subln_g": vec(subln_g),
    }
    lw = (conv_w, vec(conv_b), jnp.concatenate([w_gate_a, w_gate_x], axis=-1).astype(BF16),
          vec(b_gate_a), vec(b_gate_x), vec(lru_lambda))

    bm, bm_resident, bk_down, bq, ts = 1024, 512, 2048, 512, 512
    q_scale = dqk ** -0.5
    tables_p = _rope_tables(jnp.arange(seq, dtype=jnp.int32), dqk)
    pos_s = jnp.broadcast_to(past_len + jnp.arange(t_new, dtype=jnp.int32), (nb,))
    tables_s = _rope_tables(pos_s, dqk)
    zero_conv = jnp.zeros((batch, taps - 1, d_lru), F32)
    zero_h = jnp.zeros((batch, d_lru), F32)
    conv_t = state_conv.swapaxes(1, 2)

    xp = x_prompt.reshape(batch * seq, d_model)
    xs = x_sample.reshape(nb, d_model)
    xnp = _rmsnorm(xp, P["norm_mix_g"], 0, bm)
    xns = _rmsnorm(xs, P["norm_mix_g"], 0, nb)
    outs_p, outs_s = [], []
    for l in range(depth):
        lambda_init = 0.8 - 0.6 * math.exp(-0.3 * l)
        last = l == depth - 1
        g_next = P["final_norm_g"] if last else P["norm_mix_g"][l + 1]
        norm_dtype = F32 if last else BF16

        qp, kp, vp, xrp, grp = _in_proj(xnp, P["w_in"], l, tables_p, bm, q_scale)
        attn_p = _prompt_attention(qp, kp, vp, P["lam"], P["subln_g"], l, batch, seq, bq, lambda_init)
        lru_p, ht_p, buf_p = _prompt_lru(xrp, grp, zero_conv, zero_h, lw, l, batch, seq, ts)
        xp, xnp = _out_proj(attn_p, lru_p, P["w_out"], xp, P["norm_mlp_g"], l, bm_resident)
        qs, ks, vs, xrs, grs = _in_proj(xns, P["w_in"], l, tables_s, nb, q_scale)
        lru_s, ht_s, buf_s = _sample_lru(xrs, grs, conv_t[l], state_h[l], lw, l)
        hmid_p, attn_s = _mlp_up_decode(xnp, P["w_up"], qs, ks, vs, cache_k, cache_v, page_table,
                                        P["lam"], P["subln_g"], l, lambda_init)
        xp, xnp = _mlp_down(hmid_p, P["w_down"], l, xp, g_next, norm_dtype, bm_resident, bk_down)
        xs, xns = _out_proj(attn_s, lru_s, P["w_out"], xs, P["norm_mlp_g"], l, nb)
        hmid_s = _mlp_up(xns, P["w_up"], l, nb, 1024)
        xs, xns = _mlp_down(hmid_s, P["w_down"], l, xs, g_next, norm_dtype, nb, bk_down)
        outs_p.append((kp, vp, ht_p, buf_p))
        outs_s.append((ks, vs, ht_s, buf_s.swapaxes(0, 1)))

    k_p, v_p, h_p, c_p = (jnp.stack(t) for t in zip(*outs_p))
    k_s, v_s, h_s, c_s = (jnp.stack(t) for t in zip(*outs_s))
    return (xnp.reshape(batch, seq, d_model),
            xns.reshape(nb, t_new, d_model),
            k_p.reshape(depth, batch, seq, n_maps, dqk),
            v_p.reshape(depth, batch, seq, N_HEADS, dv),
            h_p.reshape(depth, batch, d_lru),
            c_p,
            k_s.reshape(depth, nb, t_new, n_maps, dqk),
            v_s.reshape(depth, nb, t_new, N_HEADS, dv),
            h_s,
            c_s)
```

```python
import functools
import math

import jax
import jax.numpy as jnp
from jax import lax
from jax.experimental import pallas as pl
from jax.experimental.pallas import tpu as pltpu

F32 = jnp.float32
BF16 = jnp.bfloat16

LANES_V7X = 128
VMEM_LIMIT_BYTES_V7X = 56 * 1024 * 1024

N_HEADS = 8
ROPE_THETA = 500000.0
ROT_FRACTION = 4
LRU_C = 8.0
EPS = 1e-6
SUBLN_EPS = 1e-5
NEG_INF = -1e30
PAGES_PER_STEP = 8


def _params(*sem):
    return pltpu.CompilerParams(dimension_semantics=sem, vmem_limit_bytes=VMEM_LIMIT_BYTES_V7X)


def _rms(y, g, eps):
    return y * lax.rsqrt(jnp.mean(y * y, axis=-1, keepdims=True) + eps) * g


def _rmsnorm_kernel(x_ref, g_ref, o_ref):
    o_ref[...] = _rms(x_ref[...], g_ref[...], EPS).astype(o_ref.dtype)


def _rmsnorm(x, g_all, layer, bm):
    m, d = x.shape
    return pl.pallas_call(
        _rmsnorm_kernel,
        grid=(m // bm,),
        in_specs=[pl.BlockSpec((bm, d), lambda i: (i, 0)),
                  pl.BlockSpec((None, 1, d), lambda i: (layer, 0, 0))],
        out_specs=pl.BlockSpec((bm, d), lambda i: (i, 0)),
        out_shape=jax.ShapeDtypeStruct((m, d), BF16),
        compiler_params=_params("arbitrary"),
        name="rmsnorm",
    )(x, g_all)


def _in_proj_qk_kernel(xn_ref, w_ref, c_ref, s1_ref, s2_ref, q_ref, k_ref, *, q_scale):
    z = jnp.dot(xn_ref[...], w_ref[...], preferred_element_type=F32)
    width = q_ref.shape[1]
    c, s1, s2 = c_ref[...], s1_ref[...], s2_ref[...]

    def rope_store(dst_ref, base, mul):
        for t in range(width // LANES_V7X):
            zt = z[:, base + t * LANES_V7X:base + (t + 1) * LANES_V7X]
            r = (zt * c + pltpu.roll(zt, LANES_V7X - 8, 1) * s1) + pltpu.roll(zt, 8, 1) * s2
            dst_ref[:, t * LANES_V7X:(t + 1) * LANES_V7X] = (r * mul).astype(dst_ref.dtype)

    rope_store(q_ref, 0, q_scale)
    rope_store(k_ref, width, 1.0)


def _in_proj_rest_kernel(xn_ref, w_ref, *out_refs):
    j = pl.program_id(1)
    z = jnp.dot(xn_ref[...], w_ref[...], preferred_element_type=F32)
    for idx, o_ref in enumerate(out_refs):
        @pl.when(j == idx)
        def _(o_ref=o_ref):
            o_ref[...] = z


def _in_proj(xn, w_all, layer, tables, bm, q_scale):
    m, d = xn.shape
    n_sec = 5
    width = w_all.shape[2] // n_sec
    c, s1, s2 = tables
    n_tbl = c.shape[0] // bm
    tbl_spec = pl.BlockSpec((bm, LANES_V7X), lambda i, j: (i % n_tbl, 0))
    x_spec = pl.BlockSpec((bm, d), lambda i, j: (i, 0))
    out_spec = pl.BlockSpec((bm, width), lambda i, j: (i, 0))
    f32_out = jax.ShapeDtypeStruct((m, width), F32)
    q, k = pl.pallas_call(
        functools.partial(_in_proj_qk_kernel, q_scale=q_scale),
        grid=(m // bm, 1),
        in_specs=[x_spec, pl.BlockSpec((None, d, 2 * width), lambda i, j: (layer, 0, 0)),
                  tbl_spec, tbl_spec, tbl_spec],
        out_specs=[out_spec] * 2,
        out_shape=[jax.ShapeDtypeStruct((m, width), BF16), f32_out],
        compiler_params=_params("arbitrary", "arbitrary"),
        name="in_proj_qk",
    )(xn, w_all, c, s1, s2)
    v, xr, gr = pl.pallas_call(
        _in_proj_rest_kernel,
        grid=(m // bm, n_sec - 2),
        in_specs=[x_spec, pl.BlockSpec((None, d, width), lambda i, j: (layer, 0, j + 2))],
        out_specs=[out_spec] * (n_sec - 2),
        out_shape=[f32_out] * (n_sec - 2),
        compiler_params=_params("arbitrary", "arbitrary"),
        name="in_proj_rest",
    )(xn, w_all)
    return q, k, v, xr, gr


def _diff_lambda(lam_ref, lambda_init):
    lv = lam_ref[...]
    l1 = jnp.sum(lv[0:1] * lv[1:2], axis=1, keepdims=True)
    l2 = jnp.sum(lv[2:3] * lv[3:4], axis=1, keepdims=True)
    return jnp.exp(l1) - jnp.exp(l2) + lambda_init


def _diff_combine(o0, o1, lam, g, lambda_init):
    d = o0 - lam * o1
    d = d * lax.rsqrt(jnp.mean(d * d, axis=-1, keepdims=True) + SUBLN_EPS)
    return d * g * (1.0 - lambda_init)


def _reduce_rows(x, op, reduce):
    rows = x.shape[0]
    while rows > 8 and rows % 16 == 0:
        rows //= 2
        x = op(x[:rows], x[rows:])
    return reduce(x, axis=0, keepdims=True)


def _attn_kernel(q_ref, k_ref, v_ref, lam_ref, g_ref, o_ref,
                 kb_ref, vt_ref, qt_ref, s0_ref, s1_ref, m_ref, l_ref, acc_ref,
                 *, bq, cw, dqk, lambda_init):
    qi = pl.program_id(2)
    nk = kb_ref.shape[0]

    @pl.when(qi == 0)
    def _():
        for j in range(nk):
            kb_ref[j] = k_ref[j * bq:(j + 1) * bq, :].astype(BF16)
            vt_ref[j] = v_ref[j * bq:(j + 1) * bq, :].T.astype(BF16)

    qt = q_ref[...].astype(F32).T
    row = lax.broadcasted_iota(jnp.int32, qt.shape, 0)
    qt_ref[:, :bq] = jnp.where(row < dqk, qt, 0.0).astype(BF16)
    qt_ref[:, bq:] = jnp.where(row >= dqk, qt, 0.0).astype(BF16)

    m_ref[...] = jnp.full(m_ref.shape, NEG_INF, F32)
    l_ref[...] = jnp.zeros(l_ref.shape, F32)
    acc_ref[...] = jnp.zeros(acc_ref.shape, F32)

    def scores(kj, s_ref):
        s_ref[...] = jnp.dot(kb_ref[kj], qt_ref[...], preferred_element_type=F32)

    def softmax_pv(kj, s_ref, masked):
        for c in range(2 * bq // cw):
            cs = slice(c * cw, (c + 1) * cw)
            q_lo = (c * cw) % bq
            klen = min(bq, q_lo + cw) if masked else bq
            s = s_ref[:klen, cs]
            if masked:
                key = lax.broadcasted_iota(jnp.int32, s.shape, 0)
                qry = (lax.broadcasted_iota(jnp.int32, s.shape, 1) + c * cw) & (bq - 1)
                s = jnp.where(key <= qry, s, NEG_INF)
            m_prev = m_ref[:, cs]
            m_new = jnp.maximum(m_prev, _reduce_rows(s, jnp.maximum, jnp.max))
            alpha = jnp.exp(m_prev - m_new)
            p = jnp.exp(s - m_new)
            l_ref[:, cs] = alpha * l_ref[:, cs] + _reduce_rows(p, jnp.add, jnp.sum)
            pv = jnp.dot(vt_ref[kj, :, :klen], p.astype(BF16), preferred_element_type=F32)
            acc_ref[:, cs] = alpha * acc_ref[:, cs] + pv
            m_ref[:, cs] = m_new

    def stage(kj, masked):
        for parity, (cur, nxt) in enumerate(((s0_ref, s1_ref), (s1_ref, s0_ref))):
            @pl.when((kj & 1) == parity)
            def _(cur=cur, nxt=nxt):
                if not masked:
                    scores(kj + 1, nxt)
                softmax_pv(kj, cur, masked)

    scores(0, s0_ref)

    def body(kj, carry):
        stage(kj, False)
        return carry

    lax.fori_loop(0, qi, body, 0)
    stage(qi, True)

    o = (acc_ref[...] / l_ref[...]).T
    lam = _diff_lambda(lam_ref, lambda_init)
    o_ref[...] = _diff_combine(o[:bq], o[bq:], lam, g_ref[...], lambda_init).astype(o_ref.dtype)


def _prompt_attention(q, k, v, lam_all, g_all, layer, batch, seq, bq, lambda_init):
    m, width = q.shape
    dv = width // N_HEADS
    dqk = dv // 2
    nq = seq // bq
    cw = min(bq, 4 * LANES_V7X)
    assert seq % bq == 0 and bq % cw == 0 and (bq & (bq - 1)) == 0
    return pl.pallas_call(
        functools.partial(_attn_kernel, bq=bq, cw=cw, dqk=dqk, lambda_init=lambda_init),
        grid=(batch, N_HEADS, nq),
        in_specs=[pl.BlockSpec((bq, dv), lambda b, h, i: (b * nq + i, h)),
                  pl.BlockSpec((seq, dv), lambda b, h, i: (b, h)),
                  pl.BlockSpec((seq, dv), lambda b, h, i: (b, h)),
                  pl.BlockSpec((None, 4, dqk), lambda b, h, i: (layer, 0, 0)),
                  pl.BlockSpec((None, 1, dv), lambda b, h, i: (layer, 0, 0))],
        out_specs=pl.BlockSpec((bq, dv), lambda b, h, i: (b * nq + i, h)),
        out_shape=jax.ShapeDtypeStruct((m, width), BF16),
        scratch_shapes=[pltpu.VMEM((nq, bq, dv), BF16),
                        pltpu.VMEM((nq, dv, bq), BF16),
                        pltpu.VMEM((dv, 2 * bq), BF16),
                        pltpu.VMEM((bq, 2 * bq), F32),
                        pltpu.VMEM((bq, 2 * bq), F32),
                        pltpu.VMEM((1, 2 * bq), F32),
                        pltpu.VMEM((1, 2 * bq), F32),
                        pltpu.VMEM((dv, 2 * bq), F32)],
        compiler_params=_params("arbitrary", "arbitrary", "arbitrary"),
        name="prompt_attention",
    )(q, k, v, lam_all, g_all)


def _mlp_up_decode_kernel(pt_ref, xn_ref, w_ref, q_ref, kn_ref, vn_ref, lam_ref, g_ref, *rest,
                          lambda_init):
    del pt_ref
    k_refs = rest[:PAGES_PER_STEP]
    v_refs = rest[PAGES_PER_STEP:2 * PAGES_PER_STEP]
    h_ref, o_ref, qm_ref, m_ref, l_ref, acc_ref = rest[2 * PAGES_PER_STEP:]
    p_idx = pl.program_id(1)
    n_maps, dqk, page = k_refs[0].shape
    n_heads = n_maps // 2
    width = n_maps * dqk

    @pl.when(p_idx == 0)
    def _():
        row = lax.broadcasted_iota(jnp.int32, (n_maps, width), 0)
        lane = lax.broadcasted_iota(jnp.int32, (n_maps, width), 1)
        own_map = (lane >= row * dqk) & (lane < (row + 1) * dqk)
        qrow = jnp.broadcast_to(q_ref[0].astype(F32), (n_maps, width))
        qm = jnp.where(own_map, qrow, 0.0)
        qm_ref[...] = qm.astype(BF16)
        kn = jnp.broadcast_to(kn_ref[0].astype(BF16).astype(F32), (n_maps, width))
        m_ref[...] = jnp.sum(qm * kn, axis=1, keepdims=True)
        l_ref[...] = jnp.ones(l_ref.shape, F32)
        vn = vn_ref[0].astype(BF16).astype(F32)
        acc_ref[pl.ds(0, n_heads, stride=2), :] = vn
        acc_ref[pl.ds(1, n_heads, stride=2), :] = vn

    qm = qm_ref[...]
    scores = []
    for kp_ref in k_refs:
        kt = kp_ref[...].reshape(width, -1).astype(BF16)
        scores.append(jnp.dot(qm, kt, preferred_element_type=F32))
    _mlp_up_kernel(xn_ref, w_ref, h_ref)
    m_prev = m_ref[...]
    m_new = m_prev
    for s in scores:
        m_new = jnp.maximum(m_new, jnp.max(s, axis=1, keepdims=True))
    alpha = jnp.exp(m_prev - m_new)
    l_new = alpha * l_ref[...]
    acc = alpha * acc_ref[...]
    own_head = lax.shift_right_logical(lax.broadcasted_iota(jnp.int32, acc.shape, 0), 1)
    for s, vp_ref in zip(scores, v_refs):
        p = jnp.exp(s - m_new)
        l_new = l_new + jnp.sum(p, axis=1, keepdims=True)
        pb = p.astype(BF16)
        for h in range(n_heads):
            vh = vp_ref[pl.ds(h, page, stride=n_heads), :].astype(BF16)
            pv = jnp.dot(pb, vh, preferred_element_type=F32)
            acc = acc + jnp.where(own_head == h, pv, 0.0)
    l_ref[...] = l_new
    acc_ref[...] = acc
    m_ref[...] = m_new

    @pl.when(p_idx == pl.num_programs(1) - 1)
    def _():
        acc_ref[...] = acc_ref[...] / l_ref[...]
        o0 = acc_ref[pl.ds(0, n_heads, stride=2), :]
        o1 = acc_ref[pl.ds(1, n_heads, stride=2), :]
        lam = _diff_lambda(lam_ref, lambda_init)
        o_ref[0] = _diff_combine(o0, o1, lam, g_ref[...], lambda_init).astype(o_ref.dtype)


def _mlp_up_decode(xn, w_all, q, k_new, v_new, cache_k, cache_v, page_table, lam_all, g_all, layer,
                   lambda_init):
    nb, width = q.shape
    depth, n_pool, page, n_maps, dqk = cache_k.shape
    n_heads, dv = cache_v.shape[-2:]
    n_pages = page_table.shape[1]
    m, d = xn.shape
    f = w_all.shape[2]
    n_steps = n_pages // PAGES_PER_STEP
    bm, bn = m // nb, f // n_steps
    assert n_pages % PAGES_PER_STEP == 0 and n_maps == 2 * n_heads and (n_heads & (n_heads - 1)) == 0
    assert m % nb == 0 and f % n_steps == 0 and bm % 16 == 0 and bn % LANES_V7X == 0
    ckt = cache_k.transpose(0, 1, 3, 4, 2)
    cv = cache_v.reshape(depth, n_pool, page * n_heads, dv)

    def page_spec(*block):
        zeros = (0,) * len(block)
        return [pl.BlockSpec((None, None) + block,
                             functools.partial(lambda b, p, pt, i: (layer, pt[b, p * PAGES_PER_STEP + i]) + zeros, i=i))
                for i in range(PAGES_PER_STEP)]

    row_spec = pl.BlockSpec((1, 1, width), lambda b, p, pt: (b, 0, 0))
    head_spec = pl.BlockSpec((1, n_heads, dv), lambda b, p, pt: (b, 0, 0))
    grid_spec = pltpu.PrefetchScalarGridSpec(
        num_scalar_prefetch=1,
        grid=(nb, n_steps),
        in_specs=[pl.BlockSpec((bm, d), lambda b, p, pt: (b, 0)),
                  pl.BlockSpec((None, d, bn), lambda b, p, pt: (layer, 0, p)),
                  row_spec, row_spec, head_spec,
                  pl.BlockSpec((None, 4, dqk), lambda b, p, pt: (layer, 0, 0)),
                  pl.BlockSpec((None, 1, dv), lambda b, p, pt: (layer, 0, 0))]
                 + page_spec(n_maps, dqk, page) + page_spec(page * n_heads, dv),
        out_specs=[pl.BlockSpec((bm, bn), lambda b, p, pt: (b, p)), head_spec],
        scratch_shapes=[pltpu.VMEM((n_maps, width), BF16),
                        pltpu.VMEM((n_maps, 1), F32),
                        pltpu.VMEM((n_maps, 1), F32),
                        pltpu.VMEM((n_maps, dv), F32)],
    )
    hmid, out = pl.pallas_call(
        functools.partial(_mlp_up_decode_kernel, lambda_init=lambda_init),
        grid_spec=grid_spec,
        out_shape=[jax.ShapeDtypeStruct((m, f), BF16),
                   jax.ShapeDtypeStruct((nb, n_heads, dv), BF16)],
        compiler_params=_params("arbitrary", "arbitrary"),
        name="mlp_up_decode",
    )(page_table, xn, w_all, q.reshape(nb, 1, width), k_new.reshape(nb, 1, width),
      v_new.reshape(nb, n_heads, dv), lam_all, g_all,
      *([ckt] * PAGES_PER_STEP), *([cv] * PAGES_PER_STEP))
    return hmid, out.reshape(nb, width)


def _lru_gates(xc, wg_ref, ba_ref, bx_ref, sp, store):
    xcb = xc.astype(BF16)
    bw = wg_ref.shape[1]
    for hb in range(wg_ref.shape[0]):
        sl = slice(hb * bw, (hb + 1) * bw)
        y = jnp.dot(xcb[:, sl], wg_ref[hb], preferred_element_type=F32)
        r = jax.nn.sigmoid(y[:, :bw] + ba_ref[:, sl])
        i = jax.nn.sigmoid(y[:, bw:] + bx_ref[:, sl])
        log_a = -LRU_C * r * sp[:, sl]
        a = jnp.exp(log_a)
        u = jnp.sqrt(-jnp.tanh(log_a) * (a * a + 1.0)) * (i * xc[:, sl])
        store(sl, a, u)


def _lru_kernel(xr_ref, gr_ref, conv0_ref, h0_ref, cw_ref, cb_ref, wg_ref, ba_ref, bx_ref, lam_ref,
                out_ref, ht_ref, cv_ref, xp_ref, a_ref, u_ref, h_ref, *, ts):
    ti = pl.program_id(1)
    taps = cw_ref.shape[0]
    pad = 8
    hist = pad - (taps - 1)

    @pl.when(ti == 0)
    def _():
        xp_ref[hist:pad, :] = conv0_ref[0]
        h_ref[...] = h0_ref[0]

    xp_ref[pad:pad + ts, :] = xr_ref[...]
    cw = cw_ref[...]
    conv = cw[0:1] * xp_ref[hist:hist + ts, :]
    for j in range(1, taps):
        conv = conv + cw[j:j + 1] * xp_ref[hist + j:hist + j + ts, :]
    xc = cb_ref[...] + conv
    sp = jax.nn.softplus(-lam_ref[...])

    def store(sl, a, u):
        a_ref[:, sl] = a
        u_ref[:, sl] = u

    _lru_gates(xc, wg_ref, ba_ref, bx_ref, sp, store)

    def step(t, h):
        h = a_ref[pl.ds(t, 1), :] * h + u_ref[pl.ds(t, 1), :]
        u_ref[pl.ds(t, 1), :] = h
        return h

    h_last = lax.fori_loop(0, ts, step, h_ref[...], unroll=8)
    h_ref[...] = h_last
    ht_ref[0] = h_last
    tail = xp_ref[hist + ts:pad + ts, :]
    cv_ref[0] = tail
    xp_ref[hist:pad, :] = tail
    out_ref[...] = (u_ref[...] * jax.nn.gelu(gr_ref[...])).astype(out_ref.dtype)


def _lru_weight_specs(layer, taps, d, nblk, bw, nargs):
    idx = {1: lambda b: (layer, 0, 0), 2: lambda b, t: (layer, 0, 0)}[nargs]
    idx4 = {1: lambda b: (layer, 0, 0, 0), 2: lambda b, t: (layer, 0, 0, 0)}[nargs]
    vec = pl.BlockSpec((None, 1, d), idx)
    return [pl.BlockSpec((None, taps, d), idx), vec,
            pl.BlockSpec((None, nblk, bw, 2 * bw), idx4), vec, vec, vec]


def _prompt_lru(xr, gr, conv0, h0, lw, layer, batch, seq, ts):
    m, d = xr.shape
    nt = seq // ts
    taps = lw[0].shape[1]
    nblk, bw = lw[2].shape[1], lw[2].shape[2]
    tile = pl.BlockSpec((ts, d), lambda b, t: (b * nt + t, 0))
    return pl.pallas_call(
        functools.partial(_lru_kernel, ts=ts),
        grid=(batch, nt),
        in_specs=[tile, tile,
                  pl.BlockSpec((1, taps - 1, d), lambda b, t: (b, 0, 0)),
                  pl.BlockSpec((1, 1, d), lambda b, t: (b, 0, 0))]
                 + _lru_weight_specs(layer, taps, d, nblk, bw, 2),
        out_specs=[tile,
                   pl.BlockSpec((1, 1, d), lambda b, t: (b, 0, 0)),
                   pl.BlockSpec((1, taps - 1, d), lambda b, t: (b, 0, 0))],
        out_shape=[jax.ShapeDtypeStruct((m, d), BF16),
                   jax.ShapeDtypeStruct((batch, 1, d), F32),
                   jax.ShapeDtypeStruct((batch, taps - 1, d), F32)],
        scratch_shapes=[pltpu.VMEM((ts + 8, d), F32),
                        pltpu.VMEM((ts, d), F32),
                        pltpu.VMEM((ts, d), F32),
                        pltpu.VMEM((1, d), F32)],
        compiler_params=_params("arbitrary", "arbitrary"),
        name="prompt_lru",
    )(xr, gr, conv0, h0.reshape(batch, 1, d), *lw)


def _sample_lru_kernel(xr_ref, gr_ref, conv_ref, h0_ref, cw_ref, cb_ref, wg_ref, ba_ref, bx_ref,
                       lam_ref, out_ref, ht_ref, cv_ref):
    taps = cw_ref.shape[0]
    xr = xr_ref[...]
    cw = cw_ref[...]
    conv = cw[0:1] * conv_ref[0]
    for j in range(1, taps - 1):
        conv = conv + cw[j:j + 1] * conv_ref[j]
    conv = conv + cw[taps - 1:taps] * xr
    xc = cb_ref[...] + conv
    sp = jax.nn.softplus(-lam_ref[...])
    h0 = h0_ref[...]
    gate = jax.nn.gelu(gr_ref[...])

    def store(sl, a, u):
        h = a * h0[:, sl] + u
        ht_ref[:, sl] = h
        out_ref[:, sl] = (h * gate[:, sl]).astype(out_ref.dtype)

    _lru_gates(xc, wg_ref, ba_ref, bx_ref, sp, store)
    for j in range(taps - 2):
        cv_ref[j] = conv_ref[j + 1]
    cv_ref[taps - 2] = xr


def _sample_lru(xr, gr, conv_t, h0, lw, layer):
    nb, d = xr.shape
    taps = lw[0].shape[1]
    nblk, bw = lw[2].shape[1], lw[2].shape[2]
    row = pl.BlockSpec((nb, d), lambda b: (0, 0))
    hist = pl.BlockSpec((taps - 1, nb, d), lambda b: (0, 0, 0))
    return pl.pallas_call(
        _sample_lru_kernel,
        grid=(1,),
        in_specs=[row, row, hist, row] + _lru_weight_specs(layer, taps, d, nblk, bw, 1),
        out_specs=[row, row, hist],
        out_shape=[jax.ShapeDtypeStruct((nb, d), BF16),
                   jax.ShapeDtypeStruct((nb, d), F32),
                   jax.ShapeDtypeStruct((taps - 1, nb, d), F32)],
        compiler_params=_params("arbitrary"),
        name="sample_lru",
    )(xr, gr, conv_t, h0, *lw)


def _out_proj_kernel(a_ref, r_ref, w_ref, x_ref, g_ref, xo_ref, xn_ref):
    half = a_ref.shape[1]
    y = jnp.dot(a_ref[...], w_ref[:half, :], preferred_element_type=F32)
    y = y + jnp.dot(r_ref[...], w_ref[half:, :], preferred_element_type=F32)
    x = x_ref[...] + y
    xo_ref[...] = x
    xn_ref[...] = _rms(x, g_ref[...], EPS).astype(xn_ref.dtype)


def _out_proj(attn, lru, w_all, x, g_all, layer, bm):
    m, half = attn.shape
    d = x.shape[1]
    return pl.pallas_call(
        _out_proj_kernel,
        grid=(m // bm,),
        in_specs=[pl.BlockSpec((bm, half), lambda i: (i, 0)),
                  pl.BlockSpec((bm, half), lambda i: (i, 0)),
                  pl.BlockSpec((None, 2 * half, d), lambda i: (layer, 0, 0)),
                  pl.BlockSpec((bm, d), lambda i: (i, 0)),
                  pl.BlockSpec((None, 1, d), lambda i: (layer, 0, 0))],
        out_specs=[pl.BlockSpec((bm, d), lambda i: (i, 0)),
                   pl.BlockSpec((bm, d), lambda i: (i, 0))],
        out_shape=[jax.ShapeDtypeStruct((m, d), F32), jax.ShapeDtypeStruct((m, d), BF16)],
        compiler_params=_params("arbitrary"),
        name="out_proj",
    )(attn, lru, w_all, x, g_all)


def _mlp_up_kernel(xn_ref, w_ref, h_ref):
    y = jnp.dot(xn_ref[...], w_ref[...], preferred_element_type=F32)
    h_ref[...] = jnp.square(jnp.maximum(y, 0.0)).astype(h_ref.dtype)


def _mlp_up(xn, w_all, layer, bm, bn):
    m, d = xn.shape
    f = w_all.shape[2]
    return pl.pallas_call(
        _mlp_up_kernel,
        grid=(m // bm, f // bn),
        in_specs=[pl.BlockSpec((bm, d), lambda i, j: (i, 0)),
                  pl.BlockSpec((None, d, bn), lambda i, j: (layer, 0, j))],
        out_specs=pl.BlockSpec((bm, bn), lambda i, j: (i, j)),
        out_shape=jax.ShapeDtypeStruct((m, f), BF16),
        compiler_params=_params("arbitrary", "arbitrary"),
        name="mlp_up",
    )(xn, w_all)


def _mlp_down_kernel(h_ref, w_ref, x_ref, g_ref, xo_ref, xn_ref):
    kk = pl.program_id(1)

    @pl.when(kk == 0)
    def _():
        xo_ref[...] = x_ref[...]

    xo_ref[...] += jnp.dot(h_ref[...], w_ref[...], preferred_element_type=F32)

    @pl.when(kk == pl.num_programs(1) - 1)
    def _():
        xn_ref[...] = _rms(xo_ref[...], g_ref[...], EPS).astype(xn_ref.dtype)


def _mlp_down(h, w_all, layer, x, g, norm_dtype, bm, bk):
    m, f = h.shape
    d = x.shape[1]
    return pl.pallas_call(
        _mlp_down_kernel,
        grid=(m // bm, f // bk),
        in_specs=[pl.BlockSpec((bm, bk), lambda i, k: (i, k)),
                  pl.BlockSpec((None, bk, d), lambda i, k: (layer, k, 0)),
                  pl.BlockSpec((bm, d), lambda i, k: (i, 0)),
                  pl.BlockSpec((1, d), lambda i, k: (0, 0))],
        out_specs=[pl.BlockSpec((bm, d), lambda i, k: (i, 0)),
                   pl.BlockSpec((bm, d), lambda i, k: (i, 0))],
        out_shape=[jax.ShapeDtypeStruct((m, d), F32), jax.ShapeDtypeStruct((m, d), norm_dtype)],
        compiler_params=_params("arbitrary", "arbitrary"),
        name="mlp_down",
    )(h, w_all, x, g)


def _rope_tables(positions, dqk):
    rot = dqk // ROT_FRACTION
    inv_freq = ROPE_THETA ** (-jnp.arange(0, rot, 2, dtype=F32) / rot)
    ang = positions.astype(F32)[:, None] * inv_freq[None, :]
    cos, sin = jnp.cos(ang), jnp.sin(ang)
    n = positions.shape[0]
    zh = jnp.zeros((n, rot // 2), F32)
    rest0 = jnp.zeros((n, dqk - rot), F32)
    c = jnp.concatenate([cos, cos, jnp.ones((n, dqk - rot), F32)], axis=1)
    s1 = jnp.concatenate([-sin, zh, rest0], axis=1)
    s2 = jnp.concatenate([zh, sin, rest0], axis=1)
    reps = LANES_V7X // dqk
    return tuple(jnp.tile(t, (1, reps)) for t in (c, s1, s2))


def kernel(x_prompt, x_sample, cache_k, cache_v, state_h, state_conv, page_table, norm_mix_g, w_in, lambda_q1, lambda_k1, lambda_q2, lambda_k2, subln_g, conv_w, conv_b, w_gate_a, b_gate_a, w_gate_x, b_gate_x, lru_lambda, w_out, norm_mlp_g, w_up, w_down, final_norm_g):
    batch, seq, d_model = x_prompt.shape
    nb, t_new, _ = x_sample.shape
    depth, _, page, n_maps, dqk = cache_k.shape
    dv = cache_v.shape[-1]
    d_lru = state_h.shape[-1]
    taps = conv_w.shape[1]
    assert t_new == 1, "the sample kernels handle one new token per sequence"
    past_len = page_table.shape[1] * page

    vec = lambda a: a.reshape(depth, 1, a.shape[-1])
    P = {
        "w_in": w_in.astype(BF16), "w_out": w_out.astype(BF16),
        "w_up": w_up.astype(BF16), "w_down": w_down.astype(BF16),
        "norm_mix_g": vec(norm_mix_g), "norm_mlp_g": vec(norm_mlp_g),
        "final_norm_g": final_norm_g.reshape(1, d_model),
        "lam": jnp.stack([lambda_q1, lambda_k1, lambda_q2, lambda_k2], axis=1),
        "subln_g": vec(subln_g),
    }
    lw = (conv_w, vec(conv_b), jnp.concatenate([w_gate_a, w_gate_x], axis=-1).astype(BF16),
          vec(b_gate_a), vec(b_gate_x), vec(lru_lambda))

    bm, bm_resident, bk_down, bq, ts = 1024, 512, 2048, 1024, 512
    q_scale = dqk ** -0.5
    tables_p = _rope_tables(jnp.arange(seq, dtype=jnp.int32), dqk)
    pos_s = jnp.broadcast_to(past_len + jnp.arange(t_new, dtype=jnp.int32), (nb,))
    tables_s = _rope_tables(pos_s, dqk)
    zero_conv = jnp.zeros((batch, taps - 1, d_lru), F32)
    zero_h = jnp.zeros((batch, d_lru), F32)
    conv_t = state_conv.swapaxes(1, 2)

    xp = x_prompt.reshape(batch * seq, d_model)
    xs = x_sample.reshape(nb, d_model)
    xnp = _rmsnorm(xp, P["norm_mix_g"], 0, bm)
    xns = _rmsnorm(xs, P["norm_mix_g"], 0, nb)
    outs_p, outs_s = [], []
    for l in range(depth):
        lambda_init = 0.8 - 0.6 * math.exp(-0.3 * l)
        last = l == depth - 1
        g_next = P["final_norm_g"] if last else P["norm_mix_g"][l + 1]
        norm_dtype = F32 if last else BF16

        qp, kp, vp, xrp, grp = _in_proj(xnp, P["w_in"], l, tables_p, bm, q_scale)
        attn_p = _prompt_attention(qp, kp, vp, P["lam"], P["subln_g"], l, batch, seq, bq, lambda_init)
        lru_p, ht_p, buf_p = _prompt_lru(xrp, grp, zero_conv, zero_h, lw, l, batch, seq, ts)
        xp, xnp = _out_proj(attn_p, lru_p, P["w_out"], xp, P["norm_mlp_g"], l, bm_resident)
        qs, ks, vs, xrs, grs = _in_proj(xns, P["w_in"], l, tables_s, nb, q_scale)
        lru_s, ht_s, buf_s = _sample_lru(xrs, grs, conv_t[l], state_h[l], lw, l)
        hmid_p, attn_s = _mlp_up_decode(xnp, P["w_up"], qs, ks, vs, cache_k, cache_v, page_table,
                                        P["lam"], P["subln_g"], l, lambda_init)
        xp, xnp = _mlp_down(hmid_p, P["w_down"], l, xp, g_next, norm_dtype, bm_resident, bk_down)
        xs, xns = _out_proj(attn_s, lru_s, P["w_out"], xs, P["norm_mlp_g"], l, nb)
        hmid_s = _mlp_up(xns, P["w_up"], l, nb, 1024)
        xs, xns = _mlp_down(hmid_s, P["w_down"], l, xs, g_next, norm_dtype, nb, bk_down)
        outs_p.append((kp, vp, ht_p, buf_p))
        outs_s.append((ks, vs, ht_s, buf_s.swapaxes(0, 1)))

    k_p, v_p, h_p, c_p = (jnp.stack(t) for t in zip(*outs_p))
    k_s, v_s, h_s, c_s = (jnp.stack(t) for t in zip(*outs_s))
    return (xnp.reshape(batch, seq, d_model),
            xns.reshape(nb, t_new, d_model),
            k_p.reshape(depth, batch, seq, n_maps, dqk),
            v_p.reshape(depth, batch, seq, N_HEADS, dv),
            h_p.reshape(depth, batch, d_lru),
            c_p,
            k_s.reshape(depth, nb, t_new, n_maps, dqk),
            v_s.reshape(depth, nb, t_new, N_HEADS, dv),
            h_s,
            c_s)
```
